```python
import jax, jax.numpy as jnp
from jax import lax
import numpy as np

D_MODEL = 1024
BATCH = 16
SEQ = 2048
DEPTH = 1

CHUNK = 64
RWKV_WIDTH = 512
HEAD_SIZE = 64
N_HEADS = RWKV_WIDTH // HEAD_SIZE
DECAY_LORA = 64
AAA_LORA = 64
GATE_LORA = 128
GN_EPS = HEAD_SIZE * 1e-5
POOL_WIDTH = 512
POOL_WINDOWS = (2, 4, 8, 16)
N_POOL_GROUPS = len(POOL_WINDOWS)
POOL_GROUP = POOL_WIDTH // N_POOL_GROUPS
N_BRANCH = 2
PROJ_WIDTH = 3 * RWKV_WIDTH + POOL_WIDTH + N_BRANCH * D_MODEL
D_FF = 2816
RMS_EPS = 1e-6

kernel_name = "macaron_gated_rwkv7_multiscale_pool_block"


def rms_norm(x, g):
    xf = x.astype(jnp.float32)
    y = xf * lax.rsqrt(jnp.mean(xf * xf, axis=-1, keepdims=True) + RMS_EPS)
    return (y * g.astype(jnp.float32)).astype(x.dtype)


def token_shift(x):
    return jnp.pad(x, ((0, 0), (1, 0), (0, 0)))[:, :-1]


def swiglu(x, w_gate, w_up, w_down):
    return (jax.nn.silu(x @ w_gate) * (x @ w_up)) @ w_down


def wkv7_recurrence(r, decay, k, v, a_vec, b_vec):
    b, s, h, n = r.shape

    def to_chunks(t):
        return t.astype(jnp.float32).transpose(1, 0, 2, 3).reshape(s // CHUNK, CHUNK, b, h, n)

    def frame_step(state, inp):
        r_t, w_t, k_t, v_t, a_t, b_t = inp
        sa = jnp.einsum('bhij,bhj->bhi', state, a_t)
        state = (state * w_t[:, :, None, :]
                 + sa[..., None] * b_t[:, :, None, :]
                 + v_t[..., None] * k_t[:, :, None, :])
        y_t = jnp.einsum('bhij,bhj->bhi', state, r_t)
        return state, y_t

    def chunk_step(state, chunk_inp):
        return lax.scan(frame_step, state, chunk_inp)

    state0 = jnp.zeros((b, h, n, n), jnp.float32)
    inputs = (to_chunks(r), to_chunks(decay), to_chunks(k), to_chunks(v),
              to_chunks(a_vec), to_chunks(b_vec))
    _, y = lax.scan(chunk_step, state0, inputs)
    return y.reshape(s, b, h, n).transpose(1, 0, 2, 3)


def rwkv7_branch(h, p_r, p_k, p_v, mu_rkv, mu_wag, w0, decay_a, decay_b, a0,
                 aaa_a, aaa_b, gate_a, gate_b, k_k, k_a, r_k, ln_x_w, ln_x_b):
    bsz, s, _ = h.shape
    f32 = jnp.float32
    r = p_r + (token_shift(p_r) - p_r) * mu_rkv[0]
    k = p_k + (token_shift(p_k) - p_k) * mu_rkv[1]
    v = p_v + (token_shift(p_v) - p_v) * mu_rkv[2]
    hx = token_shift(h) - h
    xw = h + hx * mu_wag[0]
    xa = h + hx * mu_wag[1]
    xg = h + hx * mu_wag[2]
    w_log = -jax.nn.softplus(-(w0 + jnp.tanh(xw @ decay_a) @ decay_b).astype(f32)) - 0.5
    decay = jnp.exp(-jnp.exp(w_log))
    a = jax.nn.sigmoid((a0 + (xa @ aaa_a) @ aaa_b).astype(f32))
    g = jax.nn.sigmoid(xg @ gate_a) @ gate_b

    heads = lambda t: t.reshape(bsz, s, N_HEADS, HEAD_SIZE)
    pheads = lambda t: t.astype(f32).reshape(N_HEADS, HEAD_SIZE)
    r_h, k_h, v_h = heads(r.astype(f32)), heads(k.astype(f32)), heads(v.astype(f32))
    a_h, w_h = heads(a), heads(decay)
    kk = k_h * pheads(k_k)
    kk = kk / jnp.maximum(jnp.sqrt(jnp.sum(kk * kk, axis=-1, keepdims=True)), 1e-12)
    k_h = k_h * (1.0 + (a_h - 1.0) * pheads(k_a))

    y = wkv7_recurrence(r_h, w_h, k_h, v_h, -kk, kk * a_h)
    mu = jnp.mean(y, axis=-1, keepdims=True)
    var = jnp.mean(jnp.square(y - mu), axis=-1, keepdims=True)
    y = (y - mu) * lax.rsqrt(var + GN_EPS)
    y = y * pheads(ln_x_w) + pheads(ln_x_b)
    bonus = jnp.sum(r_h * k_h * r_k.astype(f32), axis=-1, keepdims=True) * v_h
    y = (y + bonus).reshape(bsz, s, RWKV_WIDTH)
    return (y * g.astype(f32)).astype(h.dtype)


def multiscale_pool_branch(p, pool_w, pool_scale):
    bsz, s, _ = p.shape
    pf = p.astype(jnp.float32).reshape(bsz, s, N_POOL_GROUPS, POOL_GROUP)
    cs = jnp.pad(jnp.cumsum(pf, axis=1), ((0, 0), (1, 0), (0, 0), (0, 0)))
    windows = jnp.array(POOL_WINDOWS, dtype=jnp.int32)
    hi = jnp.arange(1, s + 1, dtype=jnp.int32)[:, None]
    lo = jnp.maximum(hi - windows[None, :], 0)
    cs_lo = cs[:, lo, jnp.arange(N_POOL_GROUPS)[None, :], :]
    count = (hi - lo).astype(jnp.float32)[None, :, :, None]
    mixed = (cs[:, 1:] - cs_lo) / count - pf
    out = jnp.einsum('bsgc,gcd->bsgd', mixed, pool_w.astype(jnp.float32))
    out = out.reshape(bsz, s, POOL_WIDTH) * pool_scale.astype(jnp.float32)
    return out.astype(p.dtype)


def _normal(key, shape, fan_in, scale=1.0):
    return scale * jax.random.normal(key, shape, jnp.float32) * (fan_in ** -0.5)


def setup_inputs(seed: int = 0) -> dict:
    key = jax.random.key(seed)
    ks = jax.random.split(key, 40)
    L, D, RW, PW = DEPTH, D_MODEL, RWKV_WIDTH, POOL_WIDTH
    nrm = lambda k, shape: jax.random.normal(k, shape, jnp.float32)
    return {
        "x": nrm(ks[0], (BATCH, SEQ, D)),
        "norm_gains": 1.0 + 0.02 * nrm(ks[1], (L, 6, D)),
        "ffn1_gate": _normal(ks[2], (L, D, D_FF), D),
        "ffn1_up": _normal(ks[3], (L, D, D_FF), D),
        "ffn1_down": _normal(ks[4], (L, D_FF, D), D_FF),
        "w_in": _normal(ks[5], (L, D, PROJ_WIDTH), D),
        "gate_bias": 0.1 * nrm(ks[6], (L, N_BRANCH, D)),
        "mu_rkv": jax.random.uniform(ks[7], (L, 3, RW), jnp.float32),
        "mu_wag": jax.random.uniform(ks[8], (L, 3, D), jnp.float32),
        "w0": jax.random.uniform(ks[9], (L, RW), jnp.float32, minval=-6.5, maxval=-1.5),
        "decay_a": _normal(ks[10], (L, D, DECAY_LORA), D),
        "decay_b": _normal(ks[11], (L, DECAY_LORA, RW), DECAY_LORA, 0.1),
        "a0": 0.1 * nrm(ks[12], (L, RW)),
        "aaa_a": _normal(ks[13], (L, D, AAA_LORA), D),
        "aaa_b": _normal(ks[14], (L, AAA_LORA, RW), AAA_LORA),
        "gate_a": _normal(ks[15], (L, D, GATE_LORA), D),
        "gate_b": _normal(ks[16], (L, GATE_LORA, RW), GATE_LORA),
        "k_k": 0.85 + 0.02 * nrm(ks[17], (L, RW)),
        "k_a": 1.0 + 0.02 * nrm(ks[18], (L, RW)),
        "r_k": 0.1 * nrm(ks[19], (L, N_HEADS, HEAD_SIZE)),
        "ln_x_w": 1.0 + 0.02 * nrm(ks[20], (L, RW)),
        "ln_x_b": 0.02 * nrm(ks[21], (L, RW)),
        "pool_w": _normal(ks[22], (L, N_POOL_GROUPS, POOL_GROUP, POOL_GROUP), POOL_GROUP),
        "pool_scale": 1.0 + 0.02 * nrm(ks[23], (L, PW)),
        "w_branch_rwkv": _normal(ks[24], (L, RW, D), RW),
        "w_branch_pool": _normal(ks[25], (L, PW, D), PW),
        "w_out": _normal(ks[26], (L, D, D), D),
        "ffn2_gate": _normal(ks[27], (L, D, D_FF), D),
        "ffn2_up": _normal(ks[28], (L, D, D_FF), D),
        "ffn2_down": _normal(ks[29], (L, D_FF, D), D_FF),
    }


def reference(x, norm_gains, ffn1_gate, ffn1_up, ffn1_down, w_in, gate_bias, mu_rkv,
              mu_wag, w0, decay_a, decay_b, a0, aaa_a, aaa_b, gate_a, gate_b, k_k, k_a,
              r_k, ln_x_w, ln_x_b, pool_w, pool_scale, w_branch_rwkv, w_branch_pool,
              w_out, ffn2_gate, ffn2_up, ffn2_down):
    bsz, s, d = x.shape
    split_at = [RWKV_WIDTH, 2 * RWKV_WIDTH, 3 * RWKV_WIDTH, 3 * RWKV_WIDTH + POOL_WIDTH]
    for l in range(DEPTH):
        g = norm_gains[l]
        f = swiglu(rms_norm(x, g[0]), ffn1_gate[l], ffn1_up[l], ffn1_down[l])
        x = x + 0.5 * rms_norm(f, g[1])

        h = rms_norm(x, g[2])
        proj = h @ w_in[l]
        p_r, p_k, p_v, p_pool, gate_logits = jnp.split(proj, split_at, axis=-1)
        gates = jax.nn.sigmoid(gate_logits.reshape(bsz, s, N_BRANCH, d) + gate_bias[l])

        y_rwkv = rwkv7_branch(h, p_r, p_k, p_v, mu_rkv[l], mu_wag[l], w0[l], decay_a[l],
                              decay_b[l], a0[l], aaa_a[l], aaa_b[l], gate_a[l], gate_b[l],
                              k_k[l], k_a[l], r_k[l], ln_x_w[l], ln_x_b[l])
        y_pool = multiscale_pool_branch(p_pool, pool_w[l], pool_scale[l])

        merged = (gates[:, :, 0] * (y_rwkv @ w_branch_rwkv[l])
                  + gates[:, :, 1] * (y_pool @ w_branch_pool[l]))
        x = x + rms_norm(merged @ w_out[l], g[3])

        f = swiglu(rms_norm(x, g[4]), ffn2_gate[l], ffn2_up[l], ffn2_down[l])
        x = x + 0.5 * rms_norm(f, g[5])
    return x
```

```python
import functools

import numpy as np
import jax
import jax.numpy as jnp
from jax import lax
from jax.experimental import pallas as pl
from jax.experimental.pallas import tpu as pltpu

F32 = jnp.float32
BF16 = jnp.bfloat16

D_MODEL = 1024
D_FF = 2816
RWKV_WIDTH = 512
HEAD_SIZE = 64
POOL_WIDTH = 512
POOL_WINDOWS = (2, 4, 8, 16)
POOL_GROUP = 128
DECAY_LORA = 64
AAA_LORA = 64
GATE_LORA = 128
LORA_WIDTH = DECAY_LORA + AAA_LORA + GATE_LORA
GN_EPS = HEAD_SIZE * 1e-5
RMS_EPS = 1e-6

CHUNK = 64
PAIR = 2 * HEAD_SIZE
N_PAIRS = RWKV_WIDTH // PAIR
HALO = 128
SHIFT_PAD = 8

FFN_ROWS = 512
MIX_ROWS = 256

C_RKV = 0
C_A2 = 3 * RWKV_WIDTH
C_SHIFT_END = C_A2 + LORA_WIDTH
C_POOL = C_SHIFT_END
C_GATE = C_POOL + POOL_WIDTH
C_A1 = C_GATE + 2 * D_MODEL
C_END = C_A1 + LORA_WIDTH

VMEM_LIMIT = 56 * 1024 * 1024


def _dot(a, b):
    return jnp.dot(a.astype(BF16), b.astype(BF16), preferred_element_type=F32)


def _dot_nt(a, b):
    return lax.dot_general(a.astype(BF16), b.astype(BF16), (((1,), (1,)), ((), ())),
                           preferred_element_type=F32)


def _dot_tn(a, b):
    return lax.dot_general(a.astype(BF16), b.astype(BF16), (((0,), (0,)), ((), ())),
                           preferred_element_type=F32)


def _rms(x, g):
    ms = jnp.mean(x * x, axis=-1, keepdims=True)
    return x * lax.rsqrt(ms + RMS_EPS) * g


def _ffn_body(x_ref, gains_ref, wg_ref, wu_ref, wd_ref, o_ref):
    x = x_ref[...]
    h = _rms(x, gains_ref[0:1, :]).astype(BF16)
    gate = jnp.dot(h, wg_ref[...], preferred_element_type=F32)
    up = jnp.dot(h, wu_ref[...], preferred_element_type=F32)
    act = (gate * jax.nn.sigmoid(gate) * up).astype(BF16)
    f = jnp.dot(act, wd_ref[...], preferred_element_type=F32)
    o_ref[...] = x + 0.5 * _rms(f, gains_ref[1:2, :])


def _const_spec(shape):
    return pl.BlockSpec(shape, lambda *_: (0,) * len(shape), pipeline_mode=pl.Buffered(1))


def _ffn(x2d, gains, wg, wu, wd):
    t, d = x2d.shape
    return pl.pallas_call(
        _ffn_body,
        grid=(t // FFN_ROWS,),
        in_specs=[
            pl.BlockSpec((FFN_ROWS, d), lambda i: (i, 0)),
            _const_spec(gains.shape),
            _const_spec(wg.shape),
            _const_spec(wu.shape),
            _const_spec(wd.shape),
        ],
        out_specs=pl.BlockSpec((FFN_ROWS, d), lambda i: (i, 0)),
        out_shape=jax.ShapeDtypeStruct((t, d), F32),
        compiler_params=pltpu.CompilerParams(
            dimension_semantics=("arbitrary",), vmem_limit_bytes=VMEM_LIMIT),
        name="macaron_ffn",
    )(x2d, gains, wg, wu, wd)


def _head_sum(x, ones_ref):
    half = ones_ref.shape[0]
    parts = [_dot(x[:, c:c + half], ones_ref[...]) for c in range(0, x.shape[1], half)]
    return jnp.concatenate(parts, axis=1)


def _masked_stack(x, m0, m1):
    return jnp.concatenate([x * m0, x * m1], axis=0)


def _wkv_chunk_pair(at, rt, bt, kt, v, wc, s_bd, tri_mask, eye, m0, m1):
    a_ms = _masked_stack(at, m0, m1)
    r_ms = _masked_stack(rt, m0, m1)
    b_ms = _masked_stack(bt, m0, m1)
    k_ms = _masked_stack(kt, m0, m1)
    v_ms = _masked_stack(v, m0, m1)
    ar = jnp.concatenate([a_ms, r_ms], axis=0)
    bk = jnp.concatenate([b_ms, k_ms], axis=0)
    o = jnp.where(tri_mask, _dot_nt(ar, bk), 0.0)
    l_bd = o[0:128, 0:128]
    arb_bd = o[128:256, 0:128]
    av = _dot(o[:, 128:256], v_ms)
    z = jnp.concatenate([a_ms, av[0:128]], axis=1)
    lk = l_bd
    n_steps = CHUNK.bit_length() - 1
    for step in range(n_steps):
        if step + 1 < n_steps:
            prod = _dot(lk, jnp.concatenate([lk, z], axis=1))
            lk = prod[:, 0:128]
            z = z + prod[:, 128:384]
        else:
            z = z + _dot(lk, z)
    pq = _dot(arb_bd, z)
    ry_ms = r_ms + pq[:, 0:128]
    yc_ms = pq[:, 128:256] + av[128:256]
    ry = ry_ms[0:64] + ry_ms[64:128]
    yc = yc_ms[0:64] + yc_ms[64:128]
    bkw = bk * wc
    rhs = jnp.concatenate(
        [jnp.concatenate([z[:, 0:128], z[:, 128:256]], axis=1),
         jnp.concatenate([jnp.zeros_like(v_ms), v_ms], axis=1)], axis=0)
    gh = _dot_tn(bkw, rhs)
    g_bd = gh[:, 0:128] + jnp.where(eye, wc, 0.0)
    h_bd = gh[:, 128:256]
    out = _dot(jnp.concatenate([g_bd, ry], axis=0), s_bd)
    y = out[128:192] + yc
    s_new = out[0:128] + h_bd
    return y, s_new


def _mix_body(x_ref, vd_ref, v5_ref, wcat_ref, wb_ref, wpool_ref, wbr_ref, wbp_ref, wout_ref,
              tri_ref, band_ref, ones_ref, o_ref,
              sh_ref, pool_ref, y_ref, st_ref):
    tm = x_ref.shape[0]
    seq_tile = pl.program_id(1)

    @pl.when(seq_tile == 0)
    def _():
        sh_ref[0:SHIFT_PAD, :] = jnp.zeros((SHIFT_PAD, sh_ref.shape[1]), F32)
        pool_ref[0:HALO, :] = jnp.zeros((HALO, POOL_WIDTH), F32)
        st_ref[...] = jnp.zeros(st_ref.shape, F32)

    x = x_ref[...]
    h = _rms(x, vd_ref[0:1, :]).astype(BF16)

    sh_ref[SHIFT_PAD:SHIFT_PAD + tm, :] = jnp.dot(
        h, wcat_ref[:, C_RKV:C_SHIFT_END], preferred_element_type=F32)
    pool_ref[HALO:HALO + tm, :] = jnp.dot(
        h, wcat_ref[:, C_POOL:C_GATE], preferred_element_type=F32)
    a1 = jnp.dot(h, wcat_ref[:, C_A1:C_END], preferred_element_type=F32)

    cur = sh_ref[SHIFT_PAD:SHIFT_PAD + tm, :]
    prv = sh_ref[SHIFT_PAD - 1:SHIFT_PAD - 1 + tm, :]
    sh_ref[SHIFT_PAD - 1:SHIFT_PAD, :] = cur[tm - 1:tm, :]

    rw = RWKV_WIDTH
    mu_r, mu_k, mu_v = v5_ref[0:1, :], v5_ref[1:2, :], v5_ref[2:3, :]
    w0, a0 = v5_ref[3:4, :], v5_ref[4:5, :]
    k_k, k_a, r_k = v5_ref[5:6, :], v5_ref[6:7, :], v5_ref[7:8, :]
    ln_w, ln_b, pool_scale = v5_ref[8:9, :], v5_ref[9:10, :], v5_ref[10:11, :]

    def lerp(c0, mu):
        c = cur[:, c0:c0 + rw]
        return c + (prv[:, c0:c0 + rw] - c) * mu

    r = lerp(0, mu_r)
    k = lerp(rw, mu_k)
    v = lerp(2 * rw, mu_v)

    lpre = a1 + prv[:, C_A2:C_SHIFT_END]
    lane = lax.broadcasted_iota(jnp.int32, (1, LORA_WIDTH), 1)
    lact = jnp.where(lane < DECAY_LORA, jnp.tanh(lpre),
                     jnp.where(lane < DECAY_LORA + AAA_LORA, lpre, jax.nn.sigmoid(lpre)))
    lora = _dot(lact, wb_ref[...])
    zw = w0 + lora[:, 0:rw]
    neg = -zw
    softplus = jnp.maximum(neg, 0.0) + jnp.log1p(jnp.exp(-jnp.abs(neg)))
    ld = -jnp.exp(-softplus - 0.5)
    a = jax.nn.sigmoid(a0 + lora[:, rw:2 * rw])
    g = lora[:, 2 * rw:3 * rw]

    kk = k * k_k
    kk = kk / jnp.maximum(jnp.sqrt(_head_sum(kk * kk, ones_ref)), 1e-12)
    k2 = k * (1.0 + (a - 1.0) * k_a)
    bvec = kk * a

    ld_hi = ld.astype(BF16)
    ld_lo = (ld - ld_hi.astype(F32)).astype(BF16)
    lc = (jnp.dot(tri_ref[...], ld_hi, preferred_element_type=F32)
          + jnp.dot(tri_ref[...], ld_lo, preferred_element_type=F32))
    e_w = jnp.exp(lc)
    e_wi = jnp.exp(-lc)
    at_all = -kk * jnp.exp(lc - ld)
    rt_all = r * e_w
    bt_all = bvec * e_wi
    kt_all = k2 * e_wi

    row = lax.broadcasted_iota(jnp.int32, (4 * CHUNK, 4 * CHUNK), 0)
    col = lax.broadcasted_iota(jnp.int32, (4 * CHUNK, 4 * CHUNK), 1)
    t_idx = row % CHUNK
    s_idx = col % CHUNK
    tri_mask = s_idx < t_idx + jnp.where(row < 2 * CHUNK, 0, 1)
    eye = (lax.broadcasted_iota(jnp.int32, (PAIR, PAIR), 0)
           == lax.broadcasted_iota(jnp.int32, (PAIR, PAIR), 1))
    lane128 = lax.broadcasted_iota(jnp.int32, (1, PAIR), 1)
    m0 = (lane128 < HEAD_SIZE).astype(F32)
    m1 = 1.0 - m0

    for p in range(N_PAIRS):
        ls = slice(p * PAIR, (p + 1) * PAIR)
        s_bd = st_ref[p]
        for c in range(tm // CHUNK):
            rs = slice(c * CHUNK, (c + 1) * CHUNK)
            wc = e_w[c * CHUNK + CHUNK - 1:c * CHUNK + CHUNK, ls]
            y, s_bd = _wkv_chunk_pair(at_all[rs, ls], rt_all[rs, ls], bt_all[rs, ls],
                                      kt_all[rs, ls], v[rs, ls], wc, s_bd,
                                      tri_mask, eye, m0, m1)
            y_ref[rs, ls] = y
        st_ref[p] = s_bd

    y = y_ref[...]
    inv_n = 1.0 / HEAD_SIZE
    mean = _head_sum(y, ones_ref) * inv_n
    yc = y - mean
    var = _head_sum(yc * yc, ones_ref) * inv_n
    yn = yc * lax.rsqrt(var + GN_EPS) * ln_w + ln_b
    bonus = _head_sum(r * k2 * r_k, ones_ref) * v
    y_rwkv = (yn + bonus) * g

    pcur = pool_ref[HALO:HALO + tm, :]
    t_abs = seq_tile * tm + lax.broadcasted_iota(jnp.int32, (tm, 1), 0)
    mixed = []
    for gi, win in enumerate(POOL_WINDOWS):
        gs = slice(gi * POOL_GROUP, (gi + 1) * POOL_GROUP)
        wsum = jnp.dot(band_ref[gi], pool_ref[:, gs].astype(BF16), preferred_element_type=F32)
        count = jnp.minimum(t_abs + 1, win).astype(F32)
        mixed.append(wsum / count - pcur[:, gs])
    mixed = jnp.concatenate(mixed, axis=1)
    pool_ref[0:HALO, :] = pool_ref[tm:tm + HALO, :]
    y_pool = _dot(mixed, wpool_ref[...]) * pool_scale

    gl0 = jnp.dot(h, wcat_ref[:, C_GATE:C_GATE + D_MODEL], preferred_element_type=F32)
    gl1 = jnp.dot(h, wcat_ref[:, C_GATE + D_MODEL:C_A1], preferred_element_type=F32)
    merged = (jax.nn.sigmoid(gl0 + vd_ref[2:3, :]) * _dot(y_rwkv, wbr_ref[...])
              + jax.nn.sigmoid(gl1 + vd_ref[3:4, :]) * _dot(y_pool, wbp_ref[...]))
    o_ref[...] = x + _rms(_dot(merged, wout_ref[...]), vd_ref[1:2, :])


def _mix(x2d, bsz, seq, vd, v5, wcat, wb, wpool, wbr, wbp, wout, tri, band, ones):
    tm = MIX_ROWS
    d = x2d.shape[1]
    tiles = seq // tm
    consts = (vd, v5, wcat, wb, wpool, wbr, wbp, wout, tri, band, ones)
    return pl.pallas_call(
        _mix_body,
        grid=(bsz, tiles),
        in_specs=[pl.BlockSpec((tm, d), lambda b, s: (b * tiles + s, 0))]
                 + [_const_spec(c.shape) for c in consts],
        out_specs=pl.BlockSpec((tm, d), lambda b, s: (b * tiles + s, 0)),
        out_shape=jax.ShapeDtypeStruct(x2d.shape, F32),
        scratch_shapes=[
            pltpu.VMEM((SHIFT_PAD + tm, C_SHIFT_END), F32),
            pltpu.VMEM((HALO + tm, POOL_WIDTH), F32),
            pltpu.VMEM((tm, RWKV_WIDTH), F32),
            pltpu.VMEM((N_PAIRS, PAIR, PAIR), F32),
        ],
        compiler_params=pltpu.CompilerParams(
            dimension_semantics=("arbitrary", "arbitrary"), vmem_limit_bytes=VMEM_LIMIT),
        name="token_mixing",
    )(x2d, *consts)


def _mix_constants(tm):
    idx = np.arange(tm)
    tri = ((idx[:, None] // CHUNK == idx[None, :] // CHUNK) & (idx[None, :] <= idx[:, None]))
    ext = np.arange(HALO + tm) - HALO
    band = np.stack([(ext[None, :] <= idx[:, None]) & (ext[None, :] > idx[:, None] - w)
                     for w in POOL_WINDOWS])
    hid = np.arange(4 * HEAD_SIZE) // HEAD_SIZE
    ones = hid[:, None] == hid[None, :]
    as_bf16 = lambda m: jnp.asarray(m.astype(np.float32), dtype=BF16)
    return as_bf16(tri), as_bf16(band), as_bf16(ones)


def _block_diag(blocks):
    n = len(blocks)
    rows = []
    for i, blk in enumerate(blocks):
        rows.append(jnp.concatenate(
            [blk if j == i else jnp.zeros((blk.shape[0], blocks[j].shape[1]), blk.dtype)
             for j in range(n)], axis=1))
    return jnp.concatenate(rows, axis=0)


def kernel(x, norm_gains, ffn1_gate, ffn1_up, ffn1_down, w_in, gate_bias, mu_rkv, mu_wag, w0,
           decay_a, decay_b, a0, aaa_a, aaa_b, gate_a, gate_b, k_k, k_a, r_k, ln_x_w, ln_x_b,
           pool_w, pool_scale, w_branch_rwkv, w_branch_pool, w_out, ffn2_gate, ffn2_up, ffn2_down):
    bsz, seq, d = x.shape
    depth = norm_gains.shape[0]
    tri, band, ones = _mix_constants(MIX_ROWS)
    x2d = x.reshape(bsz * seq, d)
    for l in range(depth):
        g = norm_gains[l]
        x2d = _ffn(x2d, g[0:2], ffn1_gate[l].astype(BF16), ffn1_up[l].astype(BF16),
                   ffn1_down[l].astype(BF16))

        lora_a = jnp.concatenate([decay_a[l], aaa_a[l], gate_a[l]], axis=1)
        mu = jnp.concatenate([jnp.broadcast_to(mu_wag[l, 0][:, None], decay_a[l].shape),
                              jnp.broadcast_to(mu_wag[l, 1][:, None], aaa_a[l].shape),
                              jnp.broadcast_to(mu_wag[l, 2][:, None], gate_a[l].shape)], axis=1)
        w_split = w_in[l]
        wcat = jnp.concatenate(
            [w_split[:, 0:3 * RWKV_WIDTH], mu * lora_a, w_split[:, 3 * RWKV_WIDTH:],
             (1.0 - mu) * lora_a], axis=1).astype(BF16)
        wb = _block_diag([decay_b[l], aaa_b[l], gate_b[l]]).astype(BF16)
        wpool = _block_diag([pool_w[l, i] for i in range(pool_w.shape[1])]).astype(BF16)
        vd = jnp.stack([g[2], g[3], gate_bias[l, 0], gate_bias[l, 1]])
        v5 = jnp.stack([mu_rkv[l, 0], mu_rkv[l, 1], mu_rkv[l, 2], w0[l], a0[l], k_k[l], k_a[l],
                        r_k[l].reshape(-1), ln_x_w[l], ln_x_b[l], pool_scale[l]])
        x2d = _mix(x2d, bsz, seq, vd, v5, wcat, wb, wpool,
                   w_branch_rwkv[l].astype(BF16), w_branch_pool[l].astype(BF16),
                   w_out[l].astype(BF16), tri, band, ones)

        x2d = _ffn(x2d, g[4:6], ffn2_gate[l].astype(BF16), ffn2_up[l].astype(BF16),
                   ffn2_down[l].astype(BF16))
    return x2d.reshape(bsz, seq, d)
```

```python
import functools

import numpy as np
import jax
import jax.numpy as jnp
from jax import lax
from jax.experimental import pallas as pl
from jax.experimental.pallas import tpu as pltpu

F32 = jnp.float32
BF16 = jnp.bfloat16

D_MODEL = 1024
D_FF = 2816
RWKV_WIDTH = 512
HEAD_SIZE = 64
POOL_WIDTH = 512
POOL_WINDOWS = (2, 4, 8, 16)
POOL_GROUP = 128
DECAY_LORA = 64
AAA_LORA = 64
GATE_LORA = 128
LORA_WIDTH = DECAY_LORA + AAA_LORA + GATE_LORA
GN_EPS = HEAD_SIZE * 1e-5
RMS_EPS = 1e-6

CHUNK = 64
PAIR = 2 * HEAD_SIZE
N_PAIRS = RWKV_WIDTH // PAIR
HALO = 128
SHIFT_PAD = 8

FFN_ROWS = 512
MIX_ROWS = 256

C_RKV = 0
C_A2 = 3 * RWKV_WIDTH
C_SHIFT_END = C_A2 + LORA_WIDTH
C_POOL = C_SHIFT_END
C_GATE = C_POOL + POOL_WIDTH
C_A1 = C_GATE + 2 * D_MODEL
C_END = C_A1 + LORA_WIDTH

VMEM_LIMIT = 56 * 1024 * 1024


def _dot(a, b):
    return jnp.dot(a.astype(BF16), b.astype(BF16), preferred_element_type=F32)


def _dot_nt(a, b):
    return lax.dot_general(a.astype(BF16), b.astype(BF16), (((1,), (1,)), ((), ())),
                           preferred_element_type=F32)


def _dot_tn(a, b):
    return lax.dot_general(a.astype(BF16), b.astype(BF16), (((0,), (0,)), ((), ())),
                           preferred_element_type=F32)


def _rms(x, g):
    ms = jnp.mean(x * x, axis=-1, keepdims=True)
    return x * lax.rsqrt(ms + RMS_EPS) * g


def _ffn_body(x_ref, gains_ref, wg_ref, wu_ref, wd_ref, o_ref):
    x = x_ref[...]
    h = _rms(x, gains_ref[0:1, :]).astype(BF16)
    gate = jnp.dot(h, wg_ref[...], preferred_element_type=F32)
    up = jnp.dot(h, wu_ref[...], preferred_element_type=F32)
    act = (gate * jax.nn.sigmoid(gate) * up).astype(BF16)
    f = jnp.dot(act, wd_ref[...], preferred_element_type=F32)
    o_ref[...] = x + 0.5 * _rms(f, gains_ref[1:2, :])


def _const_spec(shape):
    return pl.BlockSpec(shape, lambda *_: (0,) * len(shape), pipeline_mode=pl.Buffered(1))


def _ffn(x2d, gains, wg, wu, wd):
    t, d = x2d.shape
    return pl.pallas_call(
        _ffn_body,
        grid=(t // FFN_ROWS,),
        in_specs=[
            pl.BlockSpec((FFN_ROWS, d), lambda i: (i, 0)),
            _const_spec(gains.shape),
            _const_spec(wg.shape),
            _const_spec(wu.shape),
            _const_spec(wd.shape),
        ],
        out_specs=pl.BlockSpec((FFN_ROWS, d), lambda i: (i, 0)),
        out_shape=jax.ShapeDtypeStruct((t, d), F32),
        compiler_params=pltpu.CompilerParams(
            dimension_semantics=("arbitrary",), vmem_limit_bytes=VMEM_LIMIT),
        name="macaron_ffn",
    )(x2d, gains, wg, wu, wd)


def _head_sum(x, ones_ref):
    half = ones_ref.shape[0]
    parts = [_dot(x[:, c:c + half], ones_ref[...]) for c in range(0, x.shape[1], half)]
    return jnp.concatenate(parts, axis=1)


def _wkv_tile(at_all, rt_all, bt_all, kt_all, v_all, e_w, st_ref, y_ref):
    n_chunks = at_all.shape[0] // CHUNK
    items = [(p, c) for p in range(N_PAIRS) for c in range(n_chunks)]

    row = lax.broadcasted_iota(jnp.int32, (4 * CHUNK, 4 * CHUNK), 0)
    col = lax.broadcasted_iota(jnp.int32, (4 * CHUNK, 4 * CHUNK), 1)
    tri_mask = (col % CHUNK) < (row % CHUNK) + jnp.where(row < 2 * CHUNK, 0, 1)
    eye = (lax.broadcasted_iota(jnp.int32, (PAIR, PAIR), 0)
           == lax.broadcasted_iota(jnp.int32, (PAIR, PAIR), 1))
    eye_f = eye.astype(F32)
    lane = lax.broadcasted_iota(jnp.int32, (1, PAIR), 1)
    m0 = (lane < HEAD_SIZE).astype(F32)
    m1 = 1.0 - m0

    def tile(x, it):
        p, c = it
        return x[c * CHUNK:(c + 1) * CHUNK, p * PAIR:(p + 1) * PAIR]

    def stack(x, it):
        t = tile(x, it)
        return jnp.concatenate([t * m0, t * m1], axis=0)

    a_ms = {it: stack(at_all, it) for it in items}
    r_ms = {it: stack(rt_all, it) for it in items}
    v_ms = {it: stack(v_all, it) for it in items}
    bk = {it: jnp.concatenate([stack(bt_all, it), stack(kt_all, it)], axis=0) for it in items}
    wc = {(p, c): e_w[c * CHUNK + CHUNK - 1:(c + 1) * CHUNK, p * PAIR:(p + 1) * PAIR]
          for (p, c) in items}

    o = {it: jnp.where(tri_mask,
                       _dot_nt(jnp.concatenate([a_ms[it], r_ms[it]], axis=0), bk[it]), 0.0)
         for it in items}
    av = {it: _dot(o[it][:, PAIR:2 * PAIR], v_ms[it]) for it in items}

    lk = {it: o[it][0:PAIR, 0:PAIR] for it in items}
    t_inv = {it: eye_f + lk[it] for it in items}
    lk = {it: _dot(lk[it], lk[it]) for it in items}
    n_sq = CHUNK.bit_length() - 2
    for step in range(n_sq):
        prod = {it: _dot(lk[it], jnp.concatenate([lk[it], t_inv[it]], axis=1)) for it in items}
        lk = {it: prod[it][:, 0:PAIR] for it in items}
        t_inv = {it: t_inv[it] + prod[it][:, PAIR:2 * PAIR] for it in items}
    t_inv = {it: t_inv[it] + _dot(lk[it], t_inv[it]) for it in items}

    z = {it: _dot(t_inv[it], jnp.concatenate([a_ms[it], av[it][0:PAIR]], axis=1)) for it in items}
    pq = {it: _dot(o[it][PAIR:2 * PAIR, 0:PAIR], z[it]) for it in items}

    def fold(x):
        return x[0:CHUNK] + x[CHUNK:2 * CHUNK]

    ry = {it: fold(r_ms[it] + pq[it][:, 0:PAIR]) for it in items}
    yc = {it: fold(pq[it][:, PAIR:2 * PAIR] + av[it][PAIR:2 * PAIR]) for it in items}

    gh = {}
    for it in items:
        rhs = jnp.concatenate(
            [z[it], jnp.concatenate([jnp.zeros_like(v_ms[it]), v_ms[it]], axis=1)], axis=0)
        gh[it] = _dot_tn(bk[it] * wc[it], rhs)
    g_bd = {it: gh[it][:, 0:PAIR] + eye_f * wc[it] for it in items}

    s_bd = [st_ref[p] for p in range(N_PAIRS)]
    for c in range(n_chunks):
        for p in range(N_PAIRS):
            it = (p, c)
            out = _dot(jnp.concatenate([g_bd[it], ry[it]], axis=0), s_bd[p])
            y_ref[c * CHUNK:(c + 1) * CHUNK, p * PAIR:(p + 1) * PAIR] = out[PAIR:] + yc[it]
            s_bd[p] = out[0:PAIR] + gh[it][:, PAIR:2 * PAIR]
    for p in range(N_PAIRS):
        st_ref[p] = s_bd[p]


def _mix_body(x_ref, vd_ref, v5_ref, wcat_ref, wb_ref, wpool_ref, wbr_ref, wbp_ref, wout_ref,
              tri_ref, band_ref, ones_ref, o_ref,
              sh_ref, pool_ref, y_ref, st_ref):
    tm = x_ref.shape[0]
    seq_tile = pl.program_id(1)

    @pl.when(seq_tile == 0)
    def _():
        sh_ref[0:SHIFT_PAD, :] = jnp.zeros((SHIFT_PAD, sh_ref.shape[1]), F32)
        pool_ref[0:HALO, :] = jnp.zeros((HALO, POOL_WIDTH), F32)
        st_ref[...] = jnp.zeros(st_ref.shape, F32)

    x = x_ref[...]
    h = _rms(x, vd_ref[0:1, :]).astype(BF16)

    sh_ref[SHIFT_PAD:SHIFT_PAD + tm, :] = jnp.dot(
        h, wcat_ref[:, C_RKV:C_SHIFT_END], preferred_element_type=F32)
    pool_ref[HALO:HALO + tm, :] = jnp.dot(
        h, wcat_ref[:, C_POOL:C_GATE], preferred_element_type=F32)
    a1 = jnp.dot(h, wcat_ref[:, C_A1:C_END], preferred_element_type=F32)

    cur = sh_ref[SHIFT_PAD:SHIFT_PAD + tm, :]
    prv = sh_ref[SHIFT_PAD - 1:SHIFT_PAD - 1 + tm, :]
    sh_ref[SHIFT_PAD - 1:SHIFT_PAD, :] = cur[tm - 1:tm, :]

    rw = RWKV_WIDTH
    mu_r, mu_k, mu_v = v5_ref[0:1, :], v5_ref[1:2, :], v5_ref[2:3, :]
    w0, a0 = v5_ref[3:4, :], v5_ref[4:5, :]
    k_k, k_a, r_k = v5_ref[5:6, :], v5_ref[6:7, :], v5_ref[7:8, :]
    ln_w, ln_b, pool_scale = v5_ref[8:9, :], v5_ref[9:10, :], v5_ref[10:11, :]

    def lerp(c0, mu):
        c = cur[:, c0:c0 + rw]
        return c + (prv[:, c0:c0 + rw] - c) * mu

    r = lerp(0, mu_r)
    k = lerp(rw, mu_k)
    v = lerp(2 * rw, mu_v)

    lpre = a1 + prv[:, C_A2:C_SHIFT_END]
    lane = lax.broadcasted_iota(jnp.int32, (1, LORA_WIDTH), 1)
    lact = jnp.where(lane < DECAY_LORA, jnp.tanh(lpre),
                     jnp.where(lane < DECAY_LORA + AAA_LORA, lpre, jax.nn.sigmoid(lpre)))
    lora = _dot(lact, wb_ref[...])
    zw = w0 + lora[:, 0:rw]
    neg = -zw
    softplus = jnp.maximum(neg, 0.0) + jnp.log1p(jnp.exp(-jnp.abs(neg)))
    ld = -jnp.exp(-softplus - 0.5)
    a = jax.nn.sigmoid(a0 + lora[:, rw:2 * rw])
    g = lora[:, 2 * rw:3 * rw]

    kk = k * k_k
    kk = kk / jnp.maximum(jnp.sqrt(_head_sum(kk * kk, ones_ref)), 1e-12)
    k2 = k * (1.0 + (a - 1.0) * k_a)
    bvec = kk * a

    ld_hi = ld.astype(BF16)
    ld_lo = (ld - ld_hi.astype(F32)).astype(BF16)
    lc = (jnp.dot(tri_ref[...], ld_hi, preferred_element_type=F32)
          + jnp.dot(tri_ref[...], ld_lo, preferred_element_type=F32))
    e_w = jnp.exp(lc)
    e_wi = jnp.exp(-lc)
    at_all = -kk * jnp.exp(lc - ld)
    rt_all = r * e_w
    bt_all = bvec * e_wi
    kt_all = k2 * e_wi

    _wkv_tile(at_all, rt_all, bt_all, kt_all, v, e_w, st_ref, y_ref)

    y = y_ref[...]
    inv_n = 1.0 / HEAD_SIZE
    mean = _head_sum(y, ones_ref) * inv_n
    yc = y - mean
    var = _head_sum(yc * yc, ones_ref) * inv_n
    yn = yc * lax.rsqrt(var + GN_EPS) * ln_w + ln_b
    bonus = _head_sum(r * k2 * r_k, ones_ref) * v
    y_rwkv = (yn + bonus) * g

    pcur = pool_ref[HALO:HALO + tm, :]
    t_abs = seq_tile * tm + lax.broadcasted_iota(jnp.int32, (tm, 1), 0)
    mixed = []
    for gi, win in enumerate(POOL_WINDOWS):
        gs = slice(gi * POOL_GROUP, (gi + 1) * POOL_GROUP)
        wsum = jnp.dot(band_ref[gi], pool_ref[:, gs].astype(BF16), preferred_element_type=F32)
        count = jnp.minimum(t_abs + 1, win).astype(F32)
        mixed.append(wsum / count - pcur[:, gs])
    mixed = jnp.concatenate(mixed, axis=1)
    pool_ref[0:HALO, :] = pool_ref[tm:tm + HALO, :]
    y_pool = _dot(mixed, wpool_ref[...]) * pool_scale

    gl0 = jnp.dot(h, wcat_ref[:, C_GATE:C_GATE + D_MODEL], preferred_element_type=F32)
    gl1 = jnp.dot(h, wcat_ref[:, C_GATE + D_MODEL:C_A1], preferred_element_type=F32)
    merged = (jax.nn.sigmoid(gl0 + vd_ref[2:3, :]) * _dot(y_rwkv, wbr_ref[...])
              + jax.nn.sigmoid(gl1 + vd_ref[3:4, :]) * _dot(y_pool, wbp_ref[...]))
    o_ref[...] = x + _rms(_dot(merged, wout_ref[...]), vd_ref[1:2, :])


def _mix(x2d, bsz, seq, vd, v5, wcat, wb, wpool, wbr, wbp, wout, tri, band, ones):
    tm = MIX_ROWS
    d = x2d.shape[1]
    tiles = seq // tm
    consts = (vd, v5, wcat, wb, wpool, wbr, wbp, wout, tri, band, ones)
    return pl.pallas_call(
        _mix_body,
        grid=(bsz, tiles),
        in_specs=[pl.BlockSpec((tm, d), lambda b, s: (b * tiles + s, 0))]
                 + [_const_spec(c.shape) for c in consts],
        out_specs=pl.BlockSpec((tm, d), lambda b, s: (b * tiles + s, 0)),
        out_shape=jax.ShapeDtypeStruct(x2d.shape, F32),
        scratch_shapes=[
            pltpu.VMEM((SHIFT_PAD + tm, C_SHIFT_END), F32),
            pltpu.VMEM((HALO + tm, POOL_WIDTH), F32),
            pltpu.VMEM((tm, RWKV_WIDTH), F32),
            pltpu.VMEM((N_PAIRS, PAIR, PAIR), F32),
        ],
        compiler_params=pltpu.CompilerParams(
            dimension_semantics=("arbitrary", "arbitrary"), vmem_limit_bytes=VMEM_LIMIT),
        name="token_mixing",
    )(x2d, *consts)


def _mix_constants(tm):
    idx = np.arange(tm)
    tri = ((idx[:, None] // CHUNK == idx[None, :] // CHUNK) & (idx[None, :] <= idx[:, None]))
    ext = np.arange(HALO + tm) - HALO
    band = np.stack([(ext[None, :] <= idx[:, None]) & (ext[None, :] > idx[:, None] - w)
                     for w in POOL_WINDOWS])
    hid = np.arange(4 * HEAD_SIZE) // HEAD_SIZE
    ones = hid[:, None] == hid[None, :]
    as_bf16 = lambda m: jnp.asarray(m.astype(np.float32), dtype=BF16)
    return as_bf16(tri), as_bf16(band), as_bf16(ones)


def _block_diag(blocks):
    n = len(blocks)
    rows = []
    for i, blk in enumerate(blocks):
        rows.append(jnp.concatenate(
            [blk if j == i else jnp.zeros((blk.shape[0], blocks[j].shape[1]), blk.dtype)
             for j in range(n)], axis=1))
    return jnp.concatenate(rows, axis=0)


def kernel(x, norm_gains, ffn1_gate, ffn1_up, ffn1_down, w_in, gate_bias, mu_rkv, mu_wag, w0,
           decay_a, decay_b, a0, aaa_a, aaa_b, gate_a, gate_b, k_k, k_a, r_k, ln_x_w, ln_x_b,
           pool_w, pool_scale, w_branch_rwkv, w_branch_pool, w_out, ffn2_gate, ffn2_up, ffn2_down):
    bsz, seq, d = x.shape
    depth = norm_gains.shape[0]
    tri, band, ones = _mix_constants(MIX_ROWS)
    x2d = x.reshape(bsz * seq, d)
    for l in range(depth):
        g = norm_gains[l]
        x2d = _ffn(x2d, g[0:2], ffn1_gate[l].astype(BF16), ffn1_up[l].astype(BF16),
                   ffn1_down[l].astype(BF16))

        lora_a = jnp.concatenate([decay_a[l], aaa_a[l], gate_a[l]], axis=1)
        mu = jnp.concatenate([jnp.broadcast_to(mu_wag[l, 0][:, None], decay_a[l].shape),
                              jnp.broadcast_to(mu_wag[l, 1][:, None], aaa_a[l].shape),
                              jnp.broadcast_to(mu_wag[l, 2][:, None], gate_a[l].shape)], axis=1)
        w_split = w_in[l]
        wcat = jnp.concatenate(
            [w_split[:, 0:3 * RWKV_WIDTH], mu * lora_a, w_split[:, 3 * RWKV_WIDTH:],
             (1.0 - mu) * lora_a], axis=1).astype(BF16)
        wb = _block_diag([decay_b[l], aaa_b[l], gate_b[l]]).astype(BF16)
        wpool = _block_diag([pool_w[l, i] for i in range(pool_w.shape[1])]).astype(BF16)
        vd = jnp.stack([g[2], g[3], gate_bias[l, 0], gate_bias[l, 1]])
        v5 = jnp.stack([mu_rkv[l, 0], mu_rkv[l, 1], mu_rkv[l, 2], w0[l], a0[l], k_k[l], k_a[l],
                        r_k[l].reshape(-1), ln_x_w[l], ln_x_b[l], pool_scale[l]])
        x2d = _mix(x2d, bsz, seq, vd, v5, wcat, wb, wpool,
                   w_branch_rwkv[l].astype(BF16), w_branch_pool[l].astype(BF16),
                   w_out[l].astype(BF16), tri, band, ones)

        x2d = _ffn(x2d, g[4:6], ffn2_gate[l].astype(BF16), ffn2_up[l].astype(BF16),
                   ffn2_down[l].astype(BF16))
    return x2d.reshape(bsz, seq, d)
```

```python
import math

import numpy as np
import jax
import jax.numpy as jnp
from jax import lax
from jax.experimental import pallas as pl
from jax.experimental.pallas import tpu as pltpu

F32 = jnp.float32
BF16 = jnp.bfloat16

D_MODEL = 1024
D_FF = 2816
RWKV_WIDTH = 512
HEAD_SIZE = 64
POOL_WIDTH = 512
POOL_WINDOWS = (2, 4, 8, 16)
POOL_GROUP = 128
DECAY_LORA = 64
AAA_LORA = 64
GATE_LORA = 128
LORA_WIDTH = DECAY_LORA + AAA_LORA + GATE_LORA
GN_EPS = HEAD_SIZE * 1e-5
RMS_EPS = 1e-6

CHUNK = 64
PAIR = 2 * HEAD_SIZE
GROUP = 4 * HEAD_SIZE
HALO = 128
HIST_ROWS = 16
SHIFT_PAD = 8

FFN_ROWS = 512
MIX_ROWS = 512
SUB_ROWS = 256

C_RKV = 0
C_A2 = 3 * RWKV_WIDTH
C_SHIFT_END = C_A2 + LORA_WIDTH
C_POOL = C_SHIFT_END
C_GATE = C_POOL + POOL_WIDTH
C_A1 = C_GATE + 2 * D_MODEL
C_END = C_A1 + LORA_WIDTH

VMEM_LIMIT = 56 * 1024 * 1024


def _dot(a, b):
    return jnp.dot(a.astype(BF16), b.astype(BF16), preferred_element_type=F32)


def _dot_nt(a, b):
    return lax.dot_general(a.astype(BF16), b.astype(BF16), (((1,), (1,)), ((), ())),
                           preferred_element_type=F32)


def _dot_tn(a, b):
    return lax.dot_general(a.astype(BF16), b.astype(BF16), (((0,), (0,)), ((), ())),
                           preferred_element_type=F32)


def _rms(x, g):
    ms = jnp.mean(x * x, axis=-1, keepdims=True)
    return x * lax.rsqrt(ms + RMS_EPS) * g


def _ffn_body(x_ref, gains_ref, wg_ref, wu_ref, wd_ref, o_ref):
    x = x_ref[...]
    h = _rms(x, gains_ref[0:1, :]).astype(BF16)
    gate = jnp.dot(h, wg_ref[...], preferred_element_type=F32)
    up = jnp.dot(h, wu_ref[...], preferred_element_type=F32)
    act = (gate * jax.nn.sigmoid(gate) * up).astype(BF16)
    f = jnp.dot(act, wd_ref[...], preferred_element_type=F32)
    o_ref[...] = x + 0.5 * _rms(f, gains_ref[1:2, :])


def _const_spec(shape):
    return pl.BlockSpec(shape, lambda *_: (0,) * len(shape), pipeline_mode=pl.Buffered(1))


def _ffn(x2d, gains, wg, wu, wd):
    t, d = x2d.shape
    return pl.pallas_call(
        _ffn_body,
        grid=(t // FFN_ROWS,),
        in_specs=[
            pl.BlockSpec((FFN_ROWS, d), lambda i: (i, 0)),
            _const_spec(gains.shape),
            _const_spec(wg.shape),
            _const_spec(wu.shape),
            _const_spec(wd.shape),
        ],
        out_specs=pl.BlockSpec((FFN_ROWS, d), lambda i: (i, 0)),
        out_shape=jax.ShapeDtypeStruct((t, d), F32),
        compiler_params=pltpu.CompilerParams(
            dimension_semantics=("arbitrary",), vmem_limit_bytes=VMEM_LIMIT),
        name="macaron_ffn",
    )(x2d, gains, wg, wu, wd)


def _head_sum(x, ones_ref):
    half = ones_ref.shape[0]
    parts = [_dot(x[:, c:c + half], ones_ref[...]) for c in range(0, x.shape[1], half)]
    return jnp.concatenate(parts, axis=1)


def _wkv_masks():
    rowg = lax.broadcasted_iota(jnp.int32, (GROUP, GROUP), 0)
    colg = lax.broadcasted_iota(jnp.int32, (GROUP, GROUP), 1)
    row = lax.broadcasted_iota(jnp.int32, (2 * CHUNK, 2 * GROUP), 0)
    col = lax.broadcasted_iota(jnp.int32, (2 * CHUNK, 2 * GROUP), 1)
    lane = lax.broadcasted_iota(jnp.int32, (1, PAIR), 1)
    m0 = (lane < HEAD_SIZE).astype(F32)
    return dict(
        bd=(rowg // HEAD_SIZE == colg // HEAD_SIZE).astype(F32).astype(BF16),
        tri=(col % CHUNK) < (row % CHUNK) + jnp.where(row < CHUNK, 0, 1),
        eye=(lax.broadcasted_iota(jnp.int32, (CHUNK, GROUP), 0)
             == lax.broadcasted_iota(jnp.int32, (CHUNK, GROUP), 1) % CHUNK).astype(F32),
        m0=jnp.concatenate([m0, m0], axis=1),
        m1=jnp.concatenate([1.0 - m0, 1.0 - m0], axis=1))


def _bd(x, masks):
    xb = x.astype(BF16)
    return jnp.concatenate([xb] * (GROUP // HEAD_SIZE), axis=0) * masks["bd"]


def _wkv_parallel(chunks, masks, fill):
    items = range(len(chunks))
    bd = lambda x: _bd(x, masks)
    eye = masks["eye"]
    a_n = [c["at"] for c in chunks]
    r_n = [c["rt"] for c in chunks]
    b_n = [c["bt"] for c in chunks]
    k_n = [c["kt"] for c in chunks]
    v_n = [c["v"] for c in chunks]
    wc = [c["wc"] for c in chunks]

    o = [jnp.where(masks["tri"],
                   _dot_nt(jnp.concatenate([a_n[i], r_n[i]], axis=0),
                           jnp.concatenate([bd(b_n[i]), bd(k_n[i])], axis=0)), 0.0)
         for i in items]
    fill()
    av = [_dot(o[i][:, GROUP:2 * GROUP], bd(v_n[i])) for i in items]
    fill()

    lk = [o[i][0:CHUNK, 0:GROUP] for i in items]
    t_inv = [eye + lk[i] for i in items]
    lk = [_dot(lk[i], bd(lk[i])) for i in items]
    fill()
    n_sq = CHUNK.bit_length() - 2
    for _ in range(n_sq):
        prod = [_dot(jnp.concatenate([lk[i], t_inv[i]], axis=0), bd(lk[i])) for i in items]
        lk = [prod[i][0:CHUNK] for i in items]
        t_inv = [t_inv[i] + prod[i][CHUNK:2 * CHUNK] for i in items]
        fill()
    t_inv = [t_inv[i] + _dot(t_inv[i], bd(lk[i])) for i in items]
    fill()

    arb_t = [_dot(o[i][CHUNK:2 * CHUNK, 0:GROUP], bd(t_inv[i])) for i in items]
    fill()
    pq = [_dot(jnp.concatenate([t_inv[i], arb_t[i]], axis=0),
               jnp.concatenate([bd(a_n[i]), bd(av[i][0:CHUNK])], axis=1)) for i in items]
    fill()

    out = []
    for i in items:
        bkw_t = jnp.concatenate([b_n[i] * wc[i], k_n[i] * wc[i]], axis=0).astype(BF16).T
        g_parts, h_parts = [], []
        for pp in range(GROUP // PAIR):
            ls = slice(pp * PAIR, (pp + 1) * PAIR)
            p_p = pq[i][0:CHUNK, ls]
            q_p = pq[i][0:CHUNK, GROUP + pp * PAIR:GROUP + (pp + 1) * PAIR]
            w_p = jnp.concatenate(
                [jnp.concatenate([p_p, q_p], axis=1),
                 jnp.concatenate([jnp.zeros_like(p_p), v_n[i][:, ls]], axis=1)], axis=0)
            gh = _dot(bkw_t[ls, :], w_p)
            sel = gh[0:HEAD_SIZE] * masks["m0"] + gh[HEAD_SIZE:PAIR] * masks["m1"]
            g_parts.append(sel[:, 0:PAIR])
            h_parts.append(sel[:, PAIR:2 * PAIR])
        out.append(dict(
            g=jnp.concatenate(g_parts, axis=1) + eye * wc[i],
            h=jnp.concatenate(h_parts, axis=1),
            ry=r_n[i] + pq[i][CHUNK:2 * CHUNK, 0:GROUP],
            yc=pq[i][CHUNK:2 * CHUNK, GROUP:2 * GROUP] + av[i][CHUNK:2 * CHUNK]))
    fill()
    return out


def _wkv_state_step(par, s_nat, masks):
    out = _dot(jnp.concatenate([par["g"], par["ry"]], axis=0), _bd(s_nat, masks))
    return out[CHUNK:2 * CHUNK] + par["yc"], out[0:CHUNK] + par["h"]


def _mix_body(x_ref, vd_ref, v5_ref, wcat_ref, wb_ref, wpool_ref, wbr_ref, wbp_ref, wout_ref,
              tri_ref, band_ref, bandh_ref, ones_ref, o_ref,
              sh_ref, pool_ref, y_ref, st_ref):
    tm = x_ref.shape[0]
    seq_tile = pl.program_id(1)

    @pl.when(seq_tile == 0)
    def _():
        sh_ref[0:SHIFT_PAD, :] = jnp.zeros((SHIFT_PAD, sh_ref.shape[1]), F32)
        pool_ref[0:HALO, :] = jnp.zeros((HALO, POOL_WIDTH), F32)
        st_ref[...] = jnp.zeros(st_ref.shape, F32)

    x = x_ref[...]
    h = _rms(x, vd_ref[0:1, :]).astype(BF16)

    sh_ref[SHIFT_PAD:SHIFT_PAD + tm, :] = jnp.dot(
        h, wcat_ref[:, C_RKV:C_SHIFT_END], preferred_element_type=F32)
    pool_ref[HALO:HALO + tm, :] = jnp.dot(
        h, wcat_ref[:, C_POOL:C_GATE], preferred_element_type=F32)
    a1 = jnp.dot(h, wcat_ref[:, C_A1:C_END], preferred_element_type=F32)

    gates = [None] * (2 * D_MODEL // GROUP)
    mixed = [None] * (tm // SUB_ROWS)
    pool_out = {}

    def gate_piece(i):
        def run():
            c0 = C_GATE + i * GROUP
            logits = jnp.dot(h, wcat_ref[:, c0:c0 + GROUP], preferred_element_type=F32)
            bias = vd_ref[2 + (i * GROUP) // D_MODEL:3 + (i * GROUP) // D_MODEL,
                          (i * GROUP) % D_MODEL:(i * GROUP) % D_MODEL + GROUP]
            gates[i] = jax.nn.sigmoid((logits + bias).astype(BF16))
        return run

    def band_piece(j):
        def run():
            r0 = j * SUB_ROWS
            pcur = pool_ref[HALO + r0:HALO + r0 + SUB_ROWS, :]
            phist = pool_ref[r0:r0 + HALO, :]
            t_abs = seq_tile * tm + r0 + lax.broadcasted_iota(jnp.int32, (SUB_ROWS, 1), 0)
            parts = []
            for gi, win in enumerate(POOL_WINDOWS):
                gs = slice(gi * POOL_GROUP, (gi + 1) * POOL_GROUP)
                wsum = jnp.dot(band_ref[gi], pcur[:, gs].astype(BF16), preferred_element_type=F32)
                whist = jnp.dot(bandh_ref[gi], phist[:, gs].astype(BF16), preferred_element_type=F32)
                wsum = jnp.concatenate([wsum[0:HIST_ROWS] + whist, wsum[HIST_ROWS:]], axis=0)
                inv_count = 1.0 / jnp.minimum(t_abs + 1, win).astype(F32)
                parts.append(wsum * inv_count - pcur[:, gs])
            mixed[j] = jnp.concatenate(parts, axis=1)
        return run

    def pool_proj():
        pool_ref[0:HALO, :] = pool_ref[tm:tm + HALO, :]
        pool_out["y"] = (_dot(jnp.concatenate(mixed, axis=0), wpool_ref[...])
                         * v5_ref[10:11, :]).astype(BF16)

    def pool_branch_piece(i):
        def run():
            half = D_MODEL // 2
            pool_out[i] = jnp.dot(pool_out["y"], wbp_ref[:, i * half:(i + 1) * half],
                                  preferred_element_type=F32).astype(BF16)
        return run

    fillers = ([band_piece(j) for j in range(tm // SUB_ROWS)] + [pool_proj]
               + [pool_branch_piece(0), pool_branch_piece(1)]
               + [gate_piece(i) for i in range(len(gates))])

    def fill(n=1):
        for _ in range(n):
            if fillers:
                fillers.pop(0)()

    rw = RWKV_WIDTH
    mu_r, mu_k, mu_v = v5_ref[0:1, :], v5_ref[1:2, :], v5_ref[2:3, :]
    w0, a0 = v5_ref[3:4, :], v5_ref[4:5, :]
    k_k, k_a, r_k = v5_ref[5:6, :], v5_ref[6:7, :], v5_ref[7:8, :]
    ln_w, ln_b = v5_ref[8:9, :], v5_ref[9:10, :]
    lane = lax.broadcasted_iota(jnp.int32, (1, LORA_WIDTH), 1)

    prep = []
    for r0 in range(0, tm, SUB_ROWS):
        cur = sh_ref[SHIFT_PAD + r0:SHIFT_PAD + r0 + SUB_ROWS, :]
        prv = sh_ref[SHIFT_PAD - 1 + r0:SHIFT_PAD - 1 + r0 + SUB_ROWS, :]

        def lerp(c0, mu):
            c = cur[:, c0:c0 + rw]
            return c + (prv[:, c0:c0 + rw] - c) * mu

        r = lerp(0, mu_r)
        k = lerp(rw, mu_k)
        v = lerp(2 * rw, mu_v)
        fill()

        lpre = a1[r0:r0 + SUB_ROWS] + prv[:, C_A2:C_SHIFT_END]
        lact = jnp.where(lane < DECAY_LORA, jnp.tanh(lpre),
                         jnp.where(lane < DECAY_LORA + AAA_LORA, lpre, jax.nn.sigmoid(lpre)))
        lora = _dot(lact, wb_ref[...])
        ld = -math.exp(-0.5) * jax.nn.sigmoid(w0 + lora[:, 0:rw])
        a = jax.nn.sigmoid(a0 + lora[:, rw:2 * rw])
        g = lora[:, 2 * rw:3 * rw]
        fill()

        kk = k * k_k
        kk = kk * lax.rsqrt(jnp.maximum(_head_sum(kk * kk, ones_ref), 1e-24))
        k2 = k * (1.0 + (a - 1.0) * k_a)
        bvec = kk * a
        fill()

        ld_hi = ld.astype(BF16)
        ld_lo = (ld - ld_hi.astype(F32)).astype(BF16)
        lc = (jnp.dot(tri_ref[...], ld_hi, preferred_element_type=F32)
              + jnp.dot(tri_ref[...], ld_lo, preferred_element_type=F32))
        e_w = jnp.exp(lc)
        e_wi = jnp.exp(-lc)
        fill()
        prep.append(dict(at=-kk * jnp.exp(lc - ld), rt=r * e_w, bt=bvec * e_wi, kt=k2 * e_wi,
                         v=v, e_w=e_w, g=g, bonus_in=r * k2 * r_k))
        fill()
    sh_ref[SHIFT_PAD - 1:SHIFT_PAD, :] = sh_ref[SHIFT_PAD - 1 + tm:SHIFT_PAD + tm, :]
    fill(len(fillers))

    masks = _wkv_masks()
    n_groups = rw // GROUP
    groups = [slice(q * GROUP, (q + 1) * GROUP) for q in range(n_groups)]
    state = [st_ref[:, gs] for gs in groups]

    def chunk_items(p):
        return [dict(at=p["at"][c0:c0 + CHUNK, gs], rt=p["rt"][c0:c0 + CHUNK, gs],
                     bt=p["bt"][c0:c0 + CHUNK, gs], kt=p["kt"][c0:c0 + CHUNK, gs],
                     v=p["v"][c0:c0 + CHUNK, gs], wc=p["e_w"][c0 + CHUNK - 1:c0 + CHUNK, gs])
                for c0 in range(0, SUB_ROWS, CHUNK) for gs in groups]

    def state_steps(wave, par):
        def step(ci):
            def run():
                for q, gs in enumerate(groups):
                    y, state[q] = _wkv_state_step(par[ci * n_groups + q], state[q], masks)
                    r0 = wave * SUB_ROWS + ci * CHUNK
                    y_ref[r0:r0 + CHUNK, gs] = y
            return run
        return [step(ci) for ci in range(SUB_ROWS // CHUNK)]

    n_gate = D_MODEL // GROUP
    gate0 = jnp.concatenate(gates[0:n_gate], axis=1)
    gate1 = jnp.concatenate(gates[n_gate:2 * n_gate], axis=1)
    y_pool = jnp.concatenate([pool_out[0], pool_out[1]], axis=1)

    def output_rows(wave):
        rows = slice(wave * SUB_ROWS, (wave + 1) * SUB_ROWS)
        p = prep[wave]
        y = y_ref[rows, :]
        inv_n = 1.0 / HEAD_SIZE
        mean = _head_sum(y, ones_ref) * inv_n
        yield
        yc = y - mean
        var = _head_sum(yc * yc, ones_ref) * inv_n
        yield
        yn = yc * lax.rsqrt(var + GN_EPS) * ln_w + ln_b
        bonus = _head_sum(p["bonus_in"], ones_ref) * p["v"]
        y_rwkv = (yn + bonus) * p["g"]
        yield
        merged = gate0[rows] * _dot(y_rwkv, wbr_ref[...]).astype(BF16) + gate1[rows] * y_pool[rows]
        yield
        o_ref[rows, :] = x_ref[rows, :] + _rms(_dot(merged, wout_ref[...]), vd_ref[1:2, :])

    pending = []

    def fill_pending():
        if pending:
            pending.pop(0)()

    n_waves = tm // SUB_ROWS
    for wave in range(n_waves):
        par = _wkv_parallel(chunk_items(prep[wave]), masks, fill_pending)
        while pending:
            fill_pending()
        pending = state_steps(wave, par)
        if wave > 0:
            for _ in output_rows(wave - 1):
                fill_pending()
    while pending:
        fill_pending()
    for q, gs in enumerate(groups):
        st_ref[:, gs] = state[q]
    for _ in output_rows(n_waves - 1):
        pass


def _mix(x2d, bsz, seq, vd, v5, wcat, wb, wpool, wbr, wbp, wout, tri, band, bandh, ones):
    tm = MIX_ROWS
    d = x2d.shape[1]
    tiles = seq // tm
    consts = (vd, v5, wcat, wb, wpool, wbr, wbp, wout, tri, band, bandh, ones)
    return pl.pallas_call(
        _mix_body,
        grid=(bsz, tiles),
        in_specs=[pl.BlockSpec((tm, d), lambda b, s: (b * tiles + s, 0))]
                 + [_const_spec(c.shape) for c in consts],
        out_specs=pl.BlockSpec((tm, d), lambda b, s: (b * tiles + s, 0)),
        out_shape=jax.ShapeDtypeStruct(x2d.shape, F32),
        scratch_shapes=[
            pltpu.VMEM((SHIFT_PAD + tm, C_SHIFT_END), F32),
            pltpu.VMEM((HALO + tm, POOL_WIDTH), F32),
            pltpu.VMEM((tm, RWKV_WIDTH), F32),
            pltpu.VMEM((HEAD_SIZE, RWKV_WIDTH), F32),
        ],
        compiler_params=pltpu.CompilerParams(
            dimension_semantics=("arbitrary", "arbitrary"), vmem_limit_bytes=VMEM_LIMIT),
        name="token_mixing",
    )(x2d, *consts)


def _mix_constants(tm):
    idx = np.arange(tm)
    tri = ((idx[:, None] // CHUNK == idx[None, :] // CHUNK) & (idx[None, :] <= idx[:, None]))
    band = np.stack([(idx[None, :] <= idx[:, None]) & (idx[None, :] > idx[:, None] - w)
                     for w in POOL_WINDOWS])
    hist = np.arange(HALO) - HALO
    first = np.arange(HIST_ROWS)
    bandh = np.stack([hist[None, :] > first[:, None] - w for w in POOL_WINDOWS])
    hid = np.arange(4 * HEAD_SIZE) // HEAD_SIZE
    ones = hid[:, None] == hid[None, :]
    as_bf16 = lambda m: jnp.asarray(m.astype(np.float32), dtype=BF16)
    return as_bf16(tri), as_bf16(band), as_bf16(bandh), as_bf16(ones)


def _block_diag(blocks):
    n = len(blocks)
    rows = []
    for i, blk in enumerate(blocks):
        rows.append(jnp.concatenate(
            [blk if j == i else jnp.zeros((blk.shape[0], blocks[j].shape[1]), blk.dtype)
             for j in range(n)], axis=1))
    return jnp.concatenate(rows, axis=0)


def kernel(x, norm_gains, ffn1_gate, ffn1_up, ffn1_down, w_in, gate_bias, mu_rkv, mu_wag, w0,
           decay_a, decay_b, a0, aaa_a, aaa_b, gate_a, gate_b, k_k, k_a, r_k, ln_x_w, ln_x_b,
           pool_w, pool_scale, w_branch_rwkv, w_branch_pool, w_out, ffn2_gate, ffn2_up, ffn2_down):
    bsz, seq, d = x.shape
    depth = norm_gains.shape[0]
    tri, band, bandh, ones = _mix_constants(SUB_ROWS)
    x2d = x.reshape(bsz * seq, d)
    for l in range(depth):
        g = norm_gains[l]
        x2d = _ffn(x2d, g[0:2], ffn1_gate[l].astype(BF16), ffn1_up[l].astype(BF16),
                   ffn1_down[l].astype(BF16))

        lora_a = jnp.concatenate([decay_a[l], aaa_a[l], gate_a[l]], axis=1)
        mu = jnp.concatenate([jnp.broadcast_to(mu_wag[l, 0][:, None], decay_a[l].shape),
                              jnp.broadcast_to(mu_wag[l, 1][:, None], aaa_a[l].shape),
                              jnp.broadcast_to(mu_wag[l, 2][:, None], gate_a[l].shape)], axis=1)
        w_split = w_in[l]
        wcat = jnp.concatenate(
            [w_split[:, 0:3 * RWKV_WIDTH], mu * lora_a, w_split[:, 3 * RWKV_WIDTH:],
             (1.0 - mu) * lora_a], axis=1).astype(BF16)
        wb = _block_diag([decay_b[l], aaa_b[l], gate_b[l]]).astype(BF16)
        wpool = _block_diag([pool_w[l, i] for i in range(pool_w.shape[1])]).astype(BF16)
        vd = jnp.stack([g[2], g[3], gate_bias[l, 0], gate_bias[l, 1]])
        v5 = jnp.stack([mu_rkv[l, 0], mu_rkv[l, 1], mu_rkv[l, 2], w0[l], a0[l], k_k[l], k_a[l],
                        r_k[l].reshape(-1), ln_x_w[l], ln_x_b[l], pool_scale[l]])
        x2d = _mix(x2d, bsz, seq, vd, v5, wcat, wb, wpool,
                   w_branch_rwkv[l].astype(BF16), w_branch_pool[l].astype(BF16),
                   w_out[l].astype(BF16), tri, band, bandh, ones)

        x2d = _ffn(x2d, g[4:6], ffn2_gate[l].astype(BF16), ffn2_up[l].astype(BF16),
                   ffn2_down[l].astype(BF16))
    return x2d.reshape(bsz, seq, d)
```

```python
import math

import numpy as np
import jax
import jax.numpy as jnp
from jax import lax
from jax.experimental import pallas as pl
from jax.experimental.pallas import tpu as pltpu

F32 = jnp.float32
BF16 = jnp.bfloat16

D_MODEL = 1024
D_FF = 2816
RWKV_WIDTH = 512
HEAD_SIZE = 64
POOL_WIDTH = 512
POOL_WINDOWS = (2, 4, 8, 16)
POOL_GROUP = 128
DECAY_LORA = 64
AAA_LORA = 64
GATE_LORA = 128
LORA_WIDTH = DECAY_LORA + AAA_LORA + GATE_LORA
GN_EPS = HEAD_SIZE * 1e-5
RMS_EPS = 1e-6

CHUNK = 64
PAIR = 2 * HEAD_SIZE
GROUP = 4 * HEAD_SIZE
HALO = 128
HIST_ROWS = 16
SHIFT_PAD = 8

FFN_ROWS = 1024
FFN_SUB = 512
FFN_COLS = 256
MIX_ROWS = 512
SUB_ROWS = 256

C_RKV = 0
C_A2 = 3 * RWKV_WIDTH
C_SHIFT_END = C_A2 + LORA_WIDTH
C_POOL = C_SHIFT_END
C_GATE = C_POOL + POOL_WIDTH
C_A1 = C_GATE + 2 * D_MODEL
C_END = C_A1 + LORA_WIDTH

VMEM_LIMIT = 56 * 1024 * 1024


def _dot(a, b):
    return jnp.dot(a.astype(BF16), b.astype(BF16), preferred_element_type=F32)


def _dot_nt(a, b):
    return lax.dot_general(a.astype(BF16), b.astype(BF16), (((1,), (1,)), ((), ())),
                           preferred_element_type=F32)


def _dot_tn(a, b):
    return lax.dot_general(a.astype(BF16), b.astype(BF16), (((0,), (0,)), ((), ())),
                           preferred_element_type=F32)


def _rms(x, g):
    ms = jnp.mean(x * x, axis=-1, keepdims=True)
    return x * lax.rsqrt(ms + RMS_EPS) * g


def _ffn_body(x_ref, gains_ref, wg_ref, wu_ref, wd_ref, o_ref):
    n_sub = x_ref.shape[0] // FFN_SUB
    d_ff = wg_ref.shape[1]

    def prologue(i):
        return _rms(x_ref[i * FFN_SUB:(i + 1) * FFN_SUB, :], gains_ref[0:1, :]).astype(BF16)

    def epilogue(i, f):
        rows = slice(i * FFN_SUB, (i + 1) * FFN_SUB)
        o_ref[rows, :] = x_ref[rows, :] + 0.5 * _rms(f, gains_ref[1:2, :])

    h = prologue(0)
    f_prev = None
    for i in range(n_sub):
        acts = []
        h_next = None
        for c0 in range(0, d_ff, FFN_COLS):
            gate = jnp.dot(h, wg_ref[:, c0:c0 + FFN_COLS], preferred_element_type=F32)
            up = jnp.dot(h, wu_ref[:, c0:c0 + FFN_COLS], preferred_element_type=F32)
            acts.append((gate * jax.nn.sigmoid(gate) * up).astype(BF16))
            if c0 == 0 and i + 1 < n_sub:
                h_next = prologue(i + 1)
            if c0 == 2 * FFN_COLS and f_prev is not None:
                epilogue(i - 1, f_prev)
        f_prev = jnp.dot(jnp.concatenate(acts, axis=1), wd_ref[...], preferred_element_type=F32)
        h = h_next
    epilogue(n_sub - 1, f_prev)


def _const_spec(shape):
    return pl.BlockSpec(shape, lambda *_: (0,) * len(shape), pipeline_mode=pl.Buffered(1))


def _ffn(x2d, gains, wg, wu, wd):
    t, d = x2d.shape
    return pl.pallas_call(
        _ffn_body,
        grid=(t // FFN_ROWS,),
        in_specs=[
            pl.BlockSpec((FFN_ROWS, d), lambda i: (i, 0)),
            _const_spec(gains.shape),
            _const_spec(wg.shape),
            _const_spec(wu.shape),
            _const_spec(wd.shape),
        ],
        out_specs=pl.BlockSpec((FFN_ROWS, d), lambda i: (i, 0)),
        out_shape=jax.ShapeDtypeStruct((t, d), F32),
        compiler_params=pltpu.CompilerParams(
            dimension_semantics=("arbitrary",), vmem_limit_bytes=VMEM_LIMIT),
        name="macaron_ffn",
    )(x2d, gains, wg, wu, wd)


def _head_sum(x, ones_ref):
    half = ones_ref.shape[0]
    parts = [_dot(x[:, c:c + half], ones_ref[...]) for c in range(0, x.shape[1], half)]
    return jnp.concatenate(parts, axis=1)


def _wkv_masks():
    rowg = lax.broadcasted_iota(jnp.int32, (GROUP, GROUP), 0)
    colg = lax.broadcasted_iota(jnp.int32, (GROUP, GROUP), 1)
    row = lax.broadcasted_iota(jnp.int32, (2 * CHUNK, 2 * GROUP), 0)
    col = lax.broadcasted_iota(jnp.int32, (2 * CHUNK, 2 * GROUP), 1)
    lane = lax.broadcasted_iota(jnp.int32, (1, PAIR), 1)
    m0 = (lane < HEAD_SIZE).astype(F32)
    return dict(
        bd=(rowg // HEAD_SIZE == colg // HEAD_SIZE).astype(F32).astype(BF16),
        tri=(col % CHUNK) < (row % CHUNK) + jnp.where(row < CHUNK, 0, 1),
        eye=(lax.broadcasted_iota(jnp.int32, (CHUNK, GROUP), 0)
             == lax.broadcasted_iota(jnp.int32, (CHUNK, GROUP), 1) % CHUNK).astype(F32),
        m0=jnp.concatenate([m0, m0], axis=1),
        m1=jnp.concatenate([1.0 - m0, 1.0 - m0], axis=1))


def _bd(x, masks):
    xb = x.astype(BF16)
    return jnp.concatenate([xb] * (GROUP // HEAD_SIZE), axis=0) * masks["bd"]


def _wkv_parallel(chunks, masks, fill):
    items = range(len(chunks))
    bd = lambda x: _bd(x, masks)
    eye = masks["eye"]
    a_n = [c["at"] for c in chunks]
    r_n = [c["rt"] for c in chunks]
    b_n = [c["bt"] for c in chunks]
    k_n = [c["kt"] for c in chunks]
    v_n = [c["v"] for c in chunks]
    wc = [c["wc"] for c in chunks]

    o = [jnp.where(masks["tri"],
                   _dot_nt(jnp.concatenate([a_n[i], r_n[i]], axis=0),
                           jnp.concatenate([bd(b_n[i]), bd(k_n[i])], axis=0)), 0.0)
         for i in items]
    fill()
    av = [_dot(o[i][:, GROUP:2 * GROUP], bd(v_n[i])) for i in items]
    fill()

    lk = [o[i][0:CHUNK, 0:GROUP] for i in items]
    t_inv = [eye + lk[i] for i in items]
    lk = [_dot(lk[i], bd(lk[i])) for i in items]
    fill()
    n_sq = CHUNK.bit_length() - 2
    for _ in range(n_sq):
        prod = [_dot(jnp.concatenate([lk[i], t_inv[i]], axis=0), bd(lk[i])) for i in items]
        lk = [prod[i][0:CHUNK] for i in items]
        t_inv = [t_inv[i] + prod[i][CHUNK:2 * CHUNK] for i in items]
        fill()
    t_inv = [t_inv[i] + _dot(t_inv[i], bd(lk[i])) for i in items]
    fill()

    arb_t = [_dot(o[i][CHUNK:2 * CHUNK, 0:GROUP], bd(t_inv[i])) for i in items]
    fill()
    pq = [_dot(jnp.concatenate([t_inv[i], arb_t[i]], axis=0),
               jnp.concatenate([bd(a_n[i]), bd(av[i][0:CHUNK])], axis=1)) for i in items]
    fill()

    out = []
    for i in items:
        bkw_t = jnp.concatenate([b_n[i] * wc[i], k_n[i] * wc[i]], axis=0).astype(BF16).T
        g_parts, h_parts = [], []
        for pp in range(GROUP // PAIR):
            ls = slice(pp * PAIR, (pp + 1) * PAIR)
            p_p = pq[i][0:CHUNK, ls]
            q_p = pq[i][0:CHUNK, GROUP + pp * PAIR:GROUP + (pp + 1) * PAIR]
            w_p = jnp.concatenate(
                [jnp.concatenate([p_p, q_p], axis=1),
                 jnp.concatenate([jnp.zeros_like(p_p), v_n[i][:, ls]], axis=1)], axis=0)
            gh = _dot(bkw_t[ls, :], w_p)
            sel = gh[0:HEAD_SIZE] * masks["m0"] + gh[HEAD_SIZE:PAIR] * masks["m1"]
            g_parts.append(sel[:, 0:PAIR])
            h_parts.append(sel[:, PAIR:2 * PAIR])
        out.append(dict(
            g=jnp.concatenate(g_parts, axis=1) + eye * wc[i],
            h=jnp.concatenate(h_parts, axis=1),
            ry=r_n[i] + pq[i][CHUNK:2 * CHUNK, 0:GROUP],
            yc=pq[i][CHUNK:2 * CHUNK, GROUP:2 * GROUP] + av[i][CHUNK:2 * CHUNK]))
    fill()
    return out


def _wkv_state_step(par, s_nat, masks):
    out = _dot(jnp.concatenate([par["g"], par["ry"]], axis=0), _bd(s_nat, masks))
    return out[CHUNK:2 * CHUNK] + par["yc"], out[0:CHUNK] + par["h"]


def _mix_body(x_ref, vd_ref, v5_ref, wcat_ref, wb_ref, wpool_ref, wbr_ref, wbp_ref, wout_ref,
              tri_ref, band_ref, bandh_ref, ones_ref, o_ref,
              sh_ref, pool_ref, y_ref, st_ref):
    tm = x_ref.shape[0]
    seq_tile = pl.program_id(1)

    @pl.when(seq_tile == 0)
    def _():
        sh_ref[0:SHIFT_PAD, :] = jnp.zeros((SHIFT_PAD, sh_ref.shape[1]), F32)
        pool_ref[0:HALO, :] = jnp.zeros((HALO, POOL_WIDTH), F32)
        st_ref[...] = jnp.zeros(st_ref.shape, F32)

    x = x_ref[...]
    h = _rms(x, vd_ref[0:1, :]).astype(BF16)

    sh_ref[SHIFT_PAD:SHIFT_PAD + tm, :] = jnp.dot(
        h, wcat_ref[:, C_RKV:C_SHIFT_END], preferred_element_type=F32)
    pool_ref[HALO:HALO + tm, :] = jnp.dot(
        h, wcat_ref[:, C_POOL:C_GATE], preferred_element_type=F32)
    a1 = jnp.dot(h, wcat_ref[:, C_A1:C_END], preferred_element_type=F32)

    gates = [None] * (2 * D_MODEL // GROUP)
    mixed = [None] * (tm // SUB_ROWS)
    pool_out = {}

    def gate_piece(i):
        def run():
            c0 = C_GATE + i * GROUP
            logits = jnp.dot(h, wcat_ref[:, c0:c0 + GROUP], preferred_element_type=F32)
            bias = vd_ref[2 + (i * GROUP) // D_MODEL:3 + (i * GROUP) // D_MODEL,
                          (i * GROUP) % D_MODEL:(i * GROUP) % D_MODEL + GROUP]
            gates[i] = jax.nn.sigmoid((logits + bias).astype(BF16))
        return run

    def band_piece(j):
        def run():
            r0 = j * SUB_ROWS
            pcur = pool_ref[HALO + r0:HALO + r0 + SUB_ROWS, :]
            phist = pool_ref[r0:r0 + HALO, :]
            t_abs = seq_tile * tm + r0 + lax.broadcasted_iota(jnp.int32, (SUB_ROWS, 1), 0)
            parts = []
            for gi, win in enumerate(POOL_WINDOWS):
                gs = slice(gi * POOL_GROUP, (gi + 1) * POOL_GROUP)
                wsum = jnp.dot(band_ref[gi], pcur[:, gs].astype(BF16), preferred_element_type=F32)
                whist = jnp.dot(bandh_ref[gi], phist[:, gs].astype(BF16), preferred_element_type=F32)
                wsum = jnp.concatenate([wsum[0:HIST_ROWS] + whist, wsum[HIST_ROWS:]], axis=0)
                inv_count = 1.0 / jnp.minimum(t_abs + 1, win).astype(F32)
                parts.append(wsum * inv_count - pcur[:, gs])
            mixed[j] = jnp.concatenate(parts, axis=1)
        return run

    def pool_proj():
        pool_ref[0:HALO, :] = pool_ref[tm:tm + HALO, :]
        pool_out["y"] = (_dot(jnp.concatenate(mixed, axis=0), wpool_ref[...])
                         * v5_ref[10:11, :]).astype(BF16)

    def pool_branch_piece(i):
        def run():
            half = D_MODEL // 2
            pool_out[i] = jnp.dot(pool_out["y"], wbp_ref[:, i * half:(i + 1) * half],
                                  preferred_element_type=F32).astype(BF16)
        return run

    fillers = ([band_piece(j) for j in range(tm // SUB_ROWS)] + [pool_proj]
               + [pool_branch_piece(0), pool_branch_piece(1)]
               + [gate_piece(i) for i in range(len(gates))])

    def fill(n=1):
        for _ in range(n):
            if fillers:
                fillers.pop(0)()

    rw = RWKV_WIDTH
    mu_r, mu_k, mu_v = v5_ref[0:1, :], v5_ref[1:2, :], v5_ref[2:3, :]
    w0, a0 = v5_ref[3:4, :], v5_ref[4:5, :]
    k_k, k_a, r_k = v5_ref[5:6, :], v5_ref[6:7, :], v5_ref[7:8, :]
    ln_w, ln_b = v5_ref[8:9, :], v5_ref[9:10, :]
    lane = lax.broadcasted_iota(jnp.int32, (1, LORA_WIDTH), 1)

    prep = []
    for r0 in range(0, tm, SUB_ROWS):
        cur = sh_ref[SHIFT_PAD + r0:SHIFT_PAD + r0 + SUB_ROWS, :]
        prv = sh_ref[SHIFT_PAD - 1 + r0:SHIFT_PAD - 1 + r0 + SUB_ROWS, :]

        def lerp(c0, mu):
            c = cur[:, c0:c0 + rw]
            return c + (prv[:, c0:c0 + rw] - c) * mu

        r = lerp(0, mu_r)
        k = lerp(rw, mu_k)
        v = lerp(2 * rw, mu_v)
        fill()

        lpre = a1[r0:r0 + SUB_ROWS] + prv[:, C_A2:C_SHIFT_END]
        lact = jnp.where(lane < DECAY_LORA, jnp.tanh(lpre),
                         jnp.where(lane < DECAY_LORA + AAA_LORA, lpre, jax.nn.sigmoid(lpre)))
        lora = _dot(lact, wb_ref[...])
        ld = -math.exp(-0.5) * jax.nn.sigmoid(w0 + lora[:, 0:rw])
        a = jax.nn.sigmoid(a0 + lora[:, rw:2 * rw])
        g = lora[:, 2 * rw:3 * rw]
        fill()

        kk = k * k_k
        kk = kk * lax.rsqrt(jnp.maximum(_head_sum(kk * kk, ones_ref), 1e-24))
        k2 = k * (1.0 + (a - 1.0) * k_a)
        bvec = kk * a
        fill()

        ld_hi = ld.astype(BF16)
        ld_lo = (ld - ld_hi.astype(F32)).astype(BF16)
        lc = (jnp.dot(tri_ref[...], ld_hi, preferred_element_type=F32)
              + jnp.dot(tri_ref[...], ld_lo, preferred_element_type=F32))
        e_w = jnp.exp(lc)
        e_wi = jnp.exp(-lc)
        fill()
        prep.append(dict(at=-kk * jnp.exp(lc - ld), rt=r * e_w, bt=bvec * e_wi, kt=k2 * e_wi,
                         v=v, e_w=e_w, g=g, bonus_in=r * k2 * r_k))
        fill()
    sh_ref[SHIFT_PAD - 1:SHIFT_PAD, :] = sh_ref[SHIFT_PAD - 1 + tm:SHIFT_PAD + tm, :]
    fill(len(fillers))

    masks = _wkv_masks()
    n_groups = rw // GROUP
    groups = [slice(q * GROUP, (q + 1) * GROUP) for q in range(n_groups)]
    state = [st_ref[:, gs] for gs in groups]

    def chunk_items(p):
        return [dict(at=p["at"][c0:c0 + CHUNK, gs], rt=p["rt"][c0:c0 + CHUNK, gs],
                     bt=p["bt"][c0:c0 + CHUNK, gs], kt=p["kt"][c0:c0 + CHUNK, gs],
                     v=p["v"][c0:c0 + CHUNK, gs], wc=p["e_w"][c0 + CHUNK - 1:c0 + CHUNK, gs])
                for c0 in range(0, SUB_ROWS, CHUNK) for gs in groups]

    def state_steps(wave, par):
        def step(ci):
            def run():
                for q, gs in enumerate(groups):
                    y, state[q] = _wkv_state_step(par[ci * n_groups + q], state[q], masks)
                    r0 = wave * SUB_ROWS + ci * CHUNK
                    y_ref[r0:r0 + CHUNK, gs] = y
            return run
        return [step(ci) for ci in range(SUB_ROWS // CHUNK)]

    n_gate = D_MODEL // GROUP
    gate0 = jnp.concatenate(gates[0:n_gate], axis=1)
    gate1 = jnp.concatenate(gates[n_gate:2 * n_gate], axis=1)
    y_pool = jnp.concatenate([pool_out[0], pool_out[1]], axis=1)

    def output_rows(wave):
        rows = slice(wave * SUB_ROWS, (wave + 1) * SUB_ROWS)
        p = prep[wave]
        y = y_ref[rows, :]
        inv_n = 1.0 / HEAD_SIZE
        mean = _head_sum(y, ones_ref) * inv_n
        yield
        yc = y - mean
        var = _head_sum(yc * yc, ones_ref) * inv_n
        yield
        yn = yc * lax.rsqrt(var + GN_EPS) * ln_w + ln_b
        bonus = _head_sum(p["bonus_in"], ones_ref) * p["v"]
        y_rwkv = (yn + bonus) * p["g"]
        yield
        merged = gate0[rows] * _dot(y_rwkv, wbr_ref[...]).astype(BF16) + gate1[rows] * y_pool[rows]
        yield
        o_ref[rows, :] = x_ref[rows, :] + _rms(_dot(merged, wout_ref[...]), vd_ref[1:2, :])

    pending = []

    def fill_pending():
        if pending:
            pending.pop(0)()

    n_waves = tm // SUB_ROWS
    for wave in range(n_waves):
        par = _wkv_parallel(chunk_items(prep[wave]), masks, fill_pending)
        while pending:
            fill_pending()
        pending = state_steps(wave, par)
        if wave > 0:
            for _ in output_rows(wave - 1):
                fill_pending()
    while pending:
        fill_pending()
    for q, gs in enumerate(groups):
        st_ref[:, gs] = state[q]
    for _ in output_rows(n_waves - 1):
        pass


def _mix(x2d, bsz, seq, vd, v5, wcat, wb, wpool, wbr, wbp, wout, tri, band, bandh, ones):
    tm = MIX_ROWS
    d = x2d.shape[1]
    tiles = seq // tm
    consts = (vd, v5, wcat, wb, wpool, wbr, wbp, wout, tri, band, bandh, ones)
    return pl.pallas_call(
        _mix_body,
        grid=(bsz, tiles),
        in_specs=[pl.BlockSpec((tm, d), lambda b, s: (b * tiles + s, 0))]
                 + [_const_spec(c.shape) for c in consts],
        out_specs=pl.BlockSpec((tm, d), lambda b, s: (b * tiles + s, 0)),
        out_shape=jax.ShapeDtypeStruct(x2d.shape, F32),
        scratch_shapes=[
            pltpu.VMEM((SHIFT_PAD + tm, C_SHIFT_END), F32),
            pltpu.VMEM((HALO + tm, POOL_WIDTH), F32),
            pltpu.VMEM((tm, RWKV_WIDTH), F32),
            pltpu.VMEM((HEAD_SIZE, RWKV_WIDTH), F32),
        ],
        compiler_params=pltpu.CompilerParams(
            dimension_semantics=("arbitrary", "arbitrary"), vmem_limit_bytes=VMEM_LIMIT),
        name="token_mixing",
    )(x2d, *consts)


def _mix_constants(tm):
    idx = np.arange(tm)
    tri = ((idx[:, None] // CHUNK == idx[None, :] // CHUNK) & (idx[None, :] <= idx[:, None]))
    band = np.stack([(idx[None, :] <= idx[:, None]) & (idx[None, :] > idx[:, None] - w)
                     for w in POOL_WINDOWS])
    hist = np.arange(HALO) - HALO
    first = np.arange(HIST_ROWS)
    bandh = np.stack([hist[None, :] > first[:, None] - w for w in POOL_WINDOWS])
    hid = np.arange(4 * HEAD_SIZE) // HEAD_SIZE
    ones = hid[:, None] == hid[None, :]
    as_bf16 = lambda m: jnp.asarray(m.astype(np.float32), dtype=BF16)
    return as_bf16(tri), as_bf16(band), as_bf16(bandh), as_bf16(ones)


def _block_diag(blocks):
    n = len(blocks)
    rows = []
    for i, blk in enumerate(blocks):
        rows.append(jnp.concatenate(
            [blk if j == i else jnp.zeros((blk.shape[0], blocks[j].shape[1]), blk.dtype)
             for j in range(n)], axis=1))
    return jnp.concatenate(rows, axis=0)


def kernel(x, norm_gains, ffn1_gate, ffn1_up, ffn1_down, w_in, gate_bias, mu_rkv, mu_wag, w0,
           decay_a, decay_b, a0, aaa_a, aaa_b, gate_a, gate_b, k_k, k_a, r_k, ln_x_w, ln_x_b,
           pool_w, pool_scale, w_branch_rwkv, w_branch_pool, w_out, ffn2_gate, ffn2_up, ffn2_down):
    bsz, seq, d = x.shape
    depth = norm_gains.shape[0]
    tri, band, bandh, ones = _mix_constants(SUB_ROWS)
    x2d = x.reshape(bsz * seq, d)
    for l in range(depth):
        g = norm_gains[l]
        x2d = _ffn(x2d, g[0:2], ffn1_gate[l].astype(BF16), ffn1_up[l].astype(BF16),
                   ffn1_down[l].astype(BF16))

        lora_a = jnp.concatenate([decay_a[l], aaa_a[l], gate_a[l]], axis=1)
        mu = jnp.concatenate([jnp.broadcast_to(mu_wag[l, 0][:, None], decay_a[l].shape),
                              jnp.broadcast_to(mu_wag[l, 1][:, None], aaa_a[l].shape),
                              jnp.broadcast_to(mu_wag[l, 2][:, None], gate_a[l].shape)], axis=1)
        w_split = w_in[l]
        wcat = jnp.concatenate(
            [w_split[:, 0:3 * RWKV_WIDTH], mu * lora_a, w_split[:, 3 * RWKV_WIDTH:],
             (1.0 - mu) * lora_a], axis=1).astype(BF16)
        wb = _block_diag([decay_b[l], aaa_b[l], gate_b[l]]).astype(BF16)
        wpool = _block_diag([pool_w[l, i] for i in range(pool_w.shape[1])]).astype(BF16)
        vd = jnp.stack([g[2], g[3], gate_bias[l, 0], gate_bias[l, 1]])
        v5 = jnp.stack([mu_rkv[l, 0], mu_rkv[l, 1], mu_rkv[l, 2], w0[l], a0[l], k_k[l], k_a[l],
                        r_k[l].reshape(-1), ln_x_w[l], ln_x_b[l], pool_scale[l]])
        x2d = _mix(x2d, bsz, seq, vd, v5, wcat, wb, wpool,
                   w_branch_rwkv[l].astype(BF16), w_branch_pool[l].astype(BF16),
                   w_out[l].astype(BF16), tri, band, bandh, ones)

        x2d = _ffn(x2d, g[4:6], ffn2_gate[l].astype(BF16), ffn2_up[l].astype(BF16),
                   ffn2_down[l].astype(BF16))
    return x2d.reshape(bsz, seq, d)
```

```python
import math

import numpy as np
import jax
import jax.numpy as jnp
from jax import lax
from jax.experimental import pallas as pl
from jax.experimental.pallas import tpu as pltpu

F32 = jnp.float32
BF16 = jnp.bfloat16

D_MODEL = 1024
D_FF = 2816
RWKV_WIDTH = 512
HEAD_SIZE = 64
POOL_WIDTH = 512
POOL_WINDOWS = (2, 4, 8, 16)
POOL_GROUP = 128
DECAY_LORA = 64
AAA_LORA = 64
GATE_LORA = 128
LORA_WIDTH = DECAY_LORA + AAA_LORA + GATE_LORA
GN_EPS = HEAD_SIZE * 1e-5
RMS_EPS = 1e-6

CHUNK = 64
PAIR = 2 * HEAD_SIZE
GROUP = 4 * HEAD_SIZE
HALO = 128
HIST_ROWS = 16
SHIFT_PAD = 8

FFN_ROWS = 1024
FFN_SUB = 512
FFN_COLS = 256
MIX_ROWS = 512
SUB_ROWS = 256

C_RKV = 0
C_A2 = 3 * RWKV_WIDTH
C_SHIFT_END = C_A2 + LORA_WIDTH
C_POOL = C_SHIFT_END
C_GATE = C_POOL + POOL_WIDTH
C_A1 = C_GATE + 2 * D_MODEL
C_END = C_A1 + LORA_WIDTH

VMEM_LIMIT = 56 * 1024 * 1024


def _dot(a, b):
    return jnp.dot(a.astype(BF16), b.astype(BF16), preferred_element_type=F32)


def _dot_nt(a, b):
    return lax.dot_general(a.astype(BF16), b.astype(BF16), (((1,), (1,)), ((), ())),
                           preferred_element_type=F32)


def _dot_tn(a, b):
    return lax.dot_general(a.astype(BF16), b.astype(BF16), (((0,), (0,)), ((), ())),
                           preferred_element_type=F32)


def _rms(x, g):
    ms = jnp.mean(x * x, axis=-1, keepdims=True)
    return x * lax.rsqrt(ms + RMS_EPS) * g


def _ffn_body(x_ref, gains_ref, wg_ref, wu_ref, wd_ref, o_ref):
    n_sub = x_ref.shape[0] // FFN_SUB
    d_ff = wg_ref.shape[1]

    def prologue(i):
        return _rms(x_ref[i * FFN_SUB:(i + 1) * FFN_SUB, :], gains_ref[0:1, :]).astype(BF16)

    def epilogue(i, f):
        rows = slice(i * FFN_SUB, (i + 1) * FFN_SUB)
        o_ref[rows, :] = x_ref[rows, :] + 0.5 * _rms(f, gains_ref[1:2, :])

    h = prologue(0)
    f_prev = None
    for i in range(n_sub):
        acts = []
        h_next = None
        for c0 in range(0, d_ff, FFN_COLS):
            gate = jnp.dot(h, wg_ref[:, c0:c0 + FFN_COLS], preferred_element_type=F32)
            up = jnp.dot(h, wu_ref[:, c0:c0 + FFN_COLS], preferred_element_type=F32)
            acts.append((gate * jax.nn.sigmoid(gate) * up).astype(BF16))
            if c0 == 0 and i + 1 < n_sub:
                h_next = prologue(i + 1)
            if c0 == 2 * FFN_COLS and f_prev is not None:
                epilogue(i - 1, f_prev)
        f_prev = jnp.dot(jnp.concatenate(acts, axis=1), wd_ref[...], preferred_element_type=F32)
        h = h_next
    epilogue(n_sub - 1, f_prev)


def _const_spec(shape):
    return pl.BlockSpec(shape, lambda *_: (0,) * len(shape), pipeline_mode=pl.Buffered(1))


def _ffn(x2d, gains, wg, wu, wd):
    t, d = x2d.shape
    return pl.pallas_call(
        _ffn_body,
        grid=(t // FFN_ROWS,),
        in_specs=[
            pl.BlockSpec((FFN_ROWS, d), lambda i: (i, 0)),
            _const_spec(gains.shape),
            _const_spec(wg.shape),
            _const_spec(wu.shape),
            _const_spec(wd.shape),
        ],
        out_specs=pl.BlockSpec((FFN_ROWS, d), lambda i: (i, 0)),
        out_shape=jax.ShapeDtypeStruct((t, d), F32),
        compiler_params=pltpu.CompilerParams(
            dimension_semantics=("arbitrary",), vmem_limit_bytes=VMEM_LIMIT),
        name="macaron_ffn",
    )(x2d, gains, wg, wu, wd)


def _head_sum(x, ones_ref):
    half = ones_ref.shape[0]
    parts = [_dot(x[:, c:c + half], ones_ref[...]) for c in range(0, x.shape[1], half)]
    return jnp.concatenate(parts, axis=1)


def _wkv_masks():
    rowg = lax.broadcasted_iota(jnp.int32, (GROUP, GROUP), 0)
    colg = lax.broadcasted_iota(jnp.int32, (GROUP, GROUP), 1)
    row = lax.broadcasted_iota(jnp.int32, (2 * CHUNK, 2 * GROUP), 0)
    col = lax.broadcasted_iota(jnp.int32, (2 * CHUNK, 2 * GROUP), 1)
    lane = lax.broadcasted_iota(jnp.int32, (1, PAIR), 1)
    m0 = (lane < HEAD_SIZE).astype(F32)
    return dict(
        bd=(rowg // HEAD_SIZE == colg // HEAD_SIZE).astype(F32).astype(BF16),
        tri=(col % CHUNK) < (row % CHUNK) + jnp.where(row < CHUNK, 0, 1),
        eye=(lax.broadcasted_iota(jnp.int32, (CHUNK, GROUP), 0)
             == lax.broadcasted_iota(jnp.int32, (CHUNK, GROUP), 1) % CHUNK).astype(F32),
        m0=jnp.concatenate([m0, m0], axis=1),
        m1=jnp.concatenate([1.0 - m0, 1.0 - m0], axis=1))


def _bd(x, masks):
    xb = x.astype(BF16)
    return jnp.concatenate([xb] * (GROUP // HEAD_SIZE), axis=0) * masks["bd"]


def _wkv_parallel(chunks, masks, fill):
    items = range(len(chunks))
    bd = lambda x: _bd(x, masks)
    eye = masks["eye"]
    a_n = [c["at"] for c in chunks]
    r_n = [c["rt"] for c in chunks]
    b_n = [c["bt"] for c in chunks]
    k_n = [c["kt"] for c in chunks]
    v_n = [c["v"] for c in chunks]
    wc = [c["wc"] for c in chunks]

    o = [jnp.where(masks["tri"],
                   _dot(jnp.concatenate([a_n[i], r_n[i]], axis=0),
                        jnp.concatenate([bd(b_n[i]).T, bd(k_n[i]).T], axis=1)), 0.0)
         for i in items]
    fill()
    av = [_dot(o[i][:, GROUP:2 * GROUP], bd(v_n[i])) for i in items]
    fill()

    lk = [o[i][0:CHUNK, 0:GROUP] for i in items]
    t_inv = [eye + lk[i] for i in items]
    lk = [_dot(lk[i], bd(lk[i])) for i in items]
    fill()
    n_sq = CHUNK.bit_length() - 2
    for _ in range(n_sq):
        prod = [_dot(jnp.concatenate([lk[i], t_inv[i]], axis=0), bd(lk[i])) for i in items]
        lk = [prod[i][0:CHUNK] for i in items]
        t_inv = [t_inv[i] + prod[i][CHUNK:2 * CHUNK] for i in items]
        fill()
    t_inv = [t_inv[i] + _dot(t_inv[i], bd(lk[i])) for i in items]
    fill()

    arb_t = [_dot(o[i][CHUNK:2 * CHUNK, 0:GROUP], bd(t_inv[i])) for i in items]
    fill()
    pq = [_dot(jnp.concatenate([t_inv[i], arb_t[i]], axis=0),
               jnp.concatenate([bd(a_n[i]), bd(av[i][0:CHUNK])], axis=1)) for i in items]
    fill()

    out = []
    for i in items:
        bkw_t = jnp.concatenate([b_n[i] * wc[i], k_n[i] * wc[i]], axis=0).astype(BF16).T
        g_parts, h_parts = [], []
        for pp in range(GROUP // PAIR):
            ls = slice(pp * PAIR, (pp + 1) * PAIR)
            p_p = pq[i][0:CHUNK, ls]
            q_p = pq[i][0:CHUNK, GROUP + pp * PAIR:GROUP + (pp + 1) * PAIR]
            w_p = jnp.concatenate(
                [jnp.concatenate([p_p, q_p], axis=1),
                 jnp.concatenate([jnp.zeros_like(p_p), v_n[i][:, ls]], axis=1)], axis=0)
            gh = _dot(bkw_t[ls, :], w_p)
            sel = gh[0:HEAD_SIZE] * masks["m0"] + gh[HEAD_SIZE:PAIR] * masks["m1"]
            g_parts.append(sel[:, 0:PAIR])
            h_parts.append(sel[:, PAIR:2 * PAIR])
        out.append(dict(
            g=jnp.concatenate(g_parts, axis=1) + eye * wc[i],
            h=jnp.concatenate(h_parts, axis=1),
            ry=r_n[i] + pq[i][CHUNK:2 * CHUNK, 0:GROUP],
            yc=pq[i][CHUNK:2 * CHUNK, GROUP:2 * GROUP] + av[i][CHUNK:2 * CHUNK]))
    fill()
    return out


def _wkv_state_step(par, s_nat, masks):
    out = _dot(jnp.concatenate([par["g"], par["ry"]], axis=0), _bd(s_nat, masks))
    return out[CHUNK:2 * CHUNK] + par["yc"], out[0:CHUNK] + par["h"]


def _mix_body(x_ref, vd_ref, v5_ref, wcat_ref, wb_ref, wpool_ref, wbr_ref, wbp_ref, wout_ref,
              tri_ref, band_ref, bandh_ref, ones_ref, o_ref,
              sh_ref, pool_ref, y_ref, st_ref):
    tm = x_ref.shape[0]
    seq_tile = pl.program_id(1)

    @pl.when(seq_tile == 0)
    def _():
        sh_ref[0:SHIFT_PAD, :] = jnp.zeros((SHIFT_PAD, sh_ref.shape[1]), F32)
        pool_ref[0:HALO, :] = jnp.zeros((HALO, POOL_WIDTH), F32)
        st_ref[...] = jnp.zeros(st_ref.shape, F32)

    h = _rms(x_ref[...], vd_ref[0:1, :]).astype(BF16)

    sh_ref[SHIFT_PAD:SHIFT_PAD + tm, :] = jnp.dot(
        h, wcat_ref[:, C_RKV:C_SHIFT_END], preferred_element_type=F32)
    pool_ref[HALO:HALO + tm, :] = jnp.dot(
        h, wcat_ref[:, C_POOL:C_GATE], preferred_element_type=F32)
    a1 = jnp.dot(h, wcat_ref[:, C_A1:C_END], preferred_element_type=F32)

    gates = [None] * (2 * D_MODEL // GROUP)
    mixed = [None] * (tm // SUB_ROWS)
    pool_out = {}

    def gate_piece(i):
        def run():
            c0 = C_GATE + i * GROUP
            logits = jnp.dot(h, wcat_ref[:, c0:c0 + GROUP], preferred_element_type=F32)
            bias = vd_ref[2 + (i * GROUP) // D_MODEL:3 + (i * GROUP) // D_MODEL,
                          (i * GROUP) % D_MODEL:(i * GROUP) % D_MODEL + GROUP]
            gates[i] = jax.nn.sigmoid((logits + bias).astype(BF16))
        return run

    def band_piece(j):
        def run():
            r0 = j * SUB_ROWS
            pcur = pool_ref[HALO + r0:HALO + r0 + SUB_ROWS, :]
            phist = pool_ref[r0:r0 + HALO, :]
            t_abs = seq_tile * tm + r0 + lax.broadcasted_iota(jnp.int32, (SUB_ROWS, 1), 0)
            parts = []
            for gi, win in enumerate(POOL_WINDOWS):
                gs = slice(gi * POOL_GROUP, (gi + 1) * POOL_GROUP)
                wsum = jnp.dot(band_ref[gi], pcur[:, gs].astype(BF16), preferred_element_type=F32)
                whist = jnp.dot(bandh_ref[gi], phist[:, gs].astype(BF16), preferred_element_type=F32)
                wsum = jnp.concatenate([wsum[0:HIST_ROWS] + whist, wsum[HIST_ROWS:]], axis=0)
                inv_count = 1.0 / jnp.minimum(t_abs + 1, win).astype(F32)
                parts.append(wsum * inv_count - pcur[:, gs])
            mixed[j] = jnp.concatenate(parts, axis=1)
        return run

    def pool_proj():
        pool_ref[0:HALO, :] = pool_ref[tm:tm + HALO, :]
        pool_out["y"] = (_dot(jnp.concatenate(mixed, axis=0), wpool_ref[...])
                         * v5_ref[10:11, :]).astype(BF16)

    def pool_branch_piece(i):
        def run():
            half = D_MODEL // 2
            pool_out[i] = jnp.dot(pool_out["y"], wbp_ref[:, i * half:(i + 1) * half],
                                  preferred_element_type=F32).astype(BF16)
        return run

    fillers = ([band_piece(j) for j in range(tm // SUB_ROWS)] + [pool_proj]
               + [pool_branch_piece(0), pool_branch_piece(1)]
               + [gate_piece(i) for i in range(len(gates))])

    def fill(n=1):
        for _ in range(n):
            if fillers:
                fillers.pop(0)()

    rw = RWKV_WIDTH
    mu_r, mu_k, mu_v = v5_ref[0:1, :], v5_ref[1:2, :], v5_ref[2:3, :]
    w0, a0 = v5_ref[3:4, :], v5_ref[4:5, :]
    k_k, k_a, r_k = v5_ref[5:6, :], v5_ref[6:7, :], v5_ref[7:8, :]
    ln_w, ln_b = v5_ref[8:9, :], v5_ref[9:10, :]
    lane = lax.broadcasted_iota(jnp.int32, (1, LORA_WIDTH), 1)

    prep = []
    for r0 in range(0, tm, SUB_ROWS):
        cur = sh_ref[SHIFT_PAD + r0:SHIFT_PAD + r0 + SUB_ROWS, :]
        prv = sh_ref[SHIFT_PAD - 1 + r0:SHIFT_PAD - 1 + r0 + SUB_ROWS, :]

        def lerp(c0, mu):
            c = cur[:, c0:c0 + rw]
            return c + (prv[:, c0:c0 + rw] - c) * mu

        r = lerp(0, mu_r)
        k = lerp(rw, mu_k)
        v = lerp(2 * rw, mu_v)
        fill()

        lpre = a1[r0:r0 + SUB_ROWS] + prv[:, C_A2:C_SHIFT_END]
        lact = jnp.where(lane < DECAY_LORA, jnp.tanh(lpre),
                         jnp.where(lane < DECAY_LORA + AAA_LORA, lpre, jax.nn.sigmoid(lpre)))
        lora = _dot(lact, wb_ref[...])
        ld = -math.exp(-0.5) * jax.nn.sigmoid(w0 + lora[:, 0:rw])
        a = jax.nn.sigmoid(a0 + lora[:, rw:2 * rw])
        g = lora[:, 2 * rw:3 * rw]
        fill()

        kk = k * k_k
        kk = kk * lax.rsqrt(jnp.maximum(_head_sum(kk * kk, ones_ref), 1e-24))
        k2 = k * (1.0 + (a - 1.0) * k_a)
        bvec = kk * a
        fill()

        ld_hi = ld.astype(BF16)
        ld_lo = (ld - ld_hi.astype(F32)).astype(BF16)
        lc = (jnp.dot(tri_ref[...], ld_hi, preferred_element_type=F32)
              + jnp.dot(tri_ref[...], ld_lo, preferred_element_type=F32))
        e_w = jnp.exp(lc)
        e_wi = jnp.exp(-lc)
        fill()
        prep.append(dict(at=-kk * jnp.exp(lc - ld), rt=r * e_w, bt=bvec * e_wi, kt=k2 * e_wi,
                         v=v, e_w=e_w, g=g, bonus_in=r * k2 * r_k))
        fill()
    sh_ref[SHIFT_PAD - 1:SHIFT_PAD, :] = sh_ref[SHIFT_PAD - 1 + tm:SHIFT_PAD + tm, :]
    fill(len(fillers))

    masks = _wkv_masks()
    n_groups = rw // GROUP
    groups = [slice(q * GROUP, (q + 1) * GROUP) for q in range(n_groups)]
    state = [st_ref[:, gs] for gs in groups]

    def chunk_items(p):
        return [dict(at=p["at"][c0:c0 + CHUNK, gs], rt=p["rt"][c0:c0 + CHUNK, gs],
                     bt=p["bt"][c0:c0 + CHUNK, gs], kt=p["kt"][c0:c0 + CHUNK, gs],
                     v=p["v"][c0:c0 + CHUNK, gs], wc=p["e_w"][c0 + CHUNK - 1:c0 + CHUNK, gs])
                for c0 in range(0, SUB_ROWS, CHUNK) for gs in groups]

    def state_steps(wave, par):
        def step(ci):
            def run():
                for q, gs in enumerate(groups):
                    y, state[q] = _wkv_state_step(par[ci * n_groups + q], state[q], masks)
                    r0 = wave * SUB_ROWS + ci * CHUNK
                    y_ref[r0:r0 + CHUNK, gs] = y
            return run
        return [step(ci) for ci in range(SUB_ROWS // CHUNK)]

    n_gate = D_MODEL // GROUP
    gate0 = jnp.concatenate(gates[0:n_gate], axis=1)
    gate1 = jnp.concatenate(gates[n_gate:2 * n_gate], axis=1)
    y_pool = jnp.concatenate([pool_out[0], pool_out[1]], axis=1)

    def output_rows(wave):
        rows = slice(wave * SUB_ROWS, (wave + 1) * SUB_ROWS)
        p = prep[wave]
        y = y_ref[rows, :]
        inv_n = 1.0 / HEAD_SIZE
        mean = _head_sum(y, ones_ref) * inv_n
        yield
        yc = y - mean
        var = _head_sum(yc * yc, ones_ref) * inv_n
        yield
        yn = yc * lax.rsqrt(var + GN_EPS) * ln_w + ln_b
        bonus = _head_sum(p["bonus_in"], ones_ref) * p["v"]
        y_rwkv = (yn + bonus) * p["g"]
        yield
        merged = gate0[rows] * _dot(y_rwkv, wbr_ref[...]).astype(BF16) + gate1[rows] * y_pool[rows]
        yield
        o_ref[rows, :] = x_ref[rows, :] + _rms(_dot(merged, wout_ref[...]), vd_ref[1:2, :])

    pending = []

    def fill_pending():
        if pending:
            pending.pop(0)()

    n_waves = tm // SUB_ROWS
    for wave in range(n_waves):
        par = _wkv_parallel(chunk_items(prep[wave]), masks, fill_pending)
        while pending:
            fill_pending()
        pending = state_steps(wave, par)
        if wave > 0:
            for _ in output_rows(wave - 1):
                fill_pending()
    while pending:
        fill_pending()
    for q, gs in enumerate(groups):
        st_ref[:, gs] = state[q]
    for _ in output_rows(n_waves - 1):
        pass


def _mix(x2d, bsz, seq, vd, v5, wcat, wb, wpool, wbr, wbp, wout, tri, band, bandh, ones):
    tm = MIX_ROWS
    d = x2d.shape[1]
    tiles = seq // tm
    consts = (vd, v5, wcat, wb, wpool, wbr, wbp, wout, tri, band, bandh, ones)
    return pl.pallas_call(
        _mix_body,
        grid=(bsz, tiles),
        in_specs=[pl.BlockSpec((tm, d), lambda b, s: (b * tiles + s, 0))]
                 + [_const_spec(c.shape) for c in consts],
        out_specs=pl.BlockSpec((tm, d), lambda b, s: (b * tiles + s, 0)),
        out_shape=jax.ShapeDtypeStruct(x2d.shape, F32),
        scratch_shapes=[
            pltpu.VMEM((SHIFT_PAD + tm, C_SHIFT_END), F32),
            pltpu.VMEM((HALO + tm, POOL_WIDTH), F32),
            pltpu.VMEM((tm, RWKV_WIDTH), F32),
            pltpu.VMEM((HEAD_SIZE, RWKV_WIDTH), F32),
        ],
        compiler_params=pltpu.CompilerParams(
            dimension_semantics=("arbitrary", "arbitrary"), vmem_limit_bytes=VMEM_LIMIT),
        name="token_mixing",
    )(x2d, *consts)


def _mix_constants(tm):
    idx = np.arange(tm)
    tri = ((idx[:, None] // CHUNK == idx[None, :] // CHUNK) & (idx[None, :] <= idx[:, None]))
    band = np.stack([(idx[None, :] <= idx[:, None]) & (idx[None, :] > idx[:, None] - w)
                     for w in POOL_WINDOWS])
    hist = np.arange(HALO) - HALO
    first = np.arange(HIST_ROWS)
    bandh = np.stack([hist[None, :] > first[:, None] - w for w in POOL_WINDOWS])
    hid = np.arange(4 * HEAD_SIZE) // HEAD_SIZE
    ones = hid[:, None] == hid[None, :]
    as_bf16 = lambda m: jnp.asarray(m.astype(np.float32), dtype=BF16)
    return as_bf16(tri), as_bf16(band), as_bf16(bandh), as_bf16(ones)


def _block_diag(blocks):
    n = len(blocks)
    rows = []
    for i, blk in enumerate(blocks):
        rows.append(jnp.concatenate(
            [blk if j == i else jnp.zeros((blk.shape[0], blocks[j].shape[1]), blk.dtype)
             for j in range(n)], axis=1))
    return jnp.concatenate(rows, axis=0)


def kernel(x, norm_gains, ffn1_gate, ffn1_up, ffn1_down, w_in, gate_bias, mu_rkv, mu_wag, w0,
           decay_a, decay_b, a0, aaa_a, aaa_b, gate_a, gate_b, k_k, k_a, r_k, ln_x_w, ln_x_b,
           pool_w, pool_scale, w_branch_rwkv, w_branch_pool, w_out, ffn2_gate, ffn2_up, ffn2_down):
    bsz, seq, d = x.shape
    depth = norm_gains.shape[0]
    tri, band, bandh, ones = _mix_constants(SUB_ROWS)
    x2d = x.reshape(bsz * seq, d)
    for l in range(depth):
        g = norm_gains[l]
        x2d = _ffn(x2d, g[0:2], ffn1_gate[l].astype(BF16), ffn1_up[l].astype(BF16),
                   ffn1_down[l].astype(BF16))

        lora_a = jnp.concatenate([decay_a[l], aaa_a[l], gate_a[l]], axis=1)
        mu = jnp.concatenate([jnp.broadcast_to(mu_wag[l, 0][:, None], decay_a[l].shape),
                              jnp.broadcast_to(mu_wag[l, 1][:, None], aaa_a[l].shape),
                              jnp.broadcast_to(mu_wag[l, 2][:, None], gate_a[l].shape)], axis=1)
        w_split = w_in[l]
        wcat = jnp.concatenate(
            [w_split[:, 0:3 * RWKV_WIDTH], mu * lora_a, w_split[:, 3 * RWKV_WIDTH:],
             (1.0 - mu) * lora_a], axis=1).astype(BF16)
        wb = _block_diag([decay_b[l], aaa_b[l], gate_b[l]]).astype(BF16)
        wpool = _block_diag([pool_w[l, i] for i in range(pool_w.shape[1])]).astype(BF16)
        vd = jnp.stack([g[2], g[3], gate_bias[l, 0], gate_bias[l, 1]])
        v5 = jnp.stack([mu_rkv[l, 0], mu_rkv[l, 1], mu_rkv[l, 2], w0[l], a0[l], k_k[l], k_a[l],
                        r_k[l].reshape(-1), ln_x_w[l], ln_x_b[l], pool_scale[l]])
        x2d = _mix(x2d, bsz, seq, vd, v5, wcat, wb, wpool,
                   w_branch_rwkv[l].astype(BF16), w_branch_pool[l].astype(BF16),
                   w_out[l].astype(BF16), tri, band, bandh, ones)

        x2d = _ffn(x2d, g[4:6], ffn2_gate[l].astype(BF16), ffn2_up[l].astype(BF16),
                   ffn2_down[l].astype(BF16))
    return x2d.reshape(bsz, seq, d)
```

```python
import functools
import math

import numpy as np
import jax
import jax.numpy as jnp
from jax import lax
from jax.experimental import pallas as pl
from jax.experimental.pallas import tpu as pltpu

F32 = jnp.float32
BF16 = jnp.bfloat16

D_MODEL = 1024
D_FF = 2816
RWKV_WIDTH = 512
HEAD_SIZE = 64
POOL_WIDTH = 512
POOL_WINDOWS = (2, 4, 8, 16)
POOL_GROUP = 128
DECAY_LORA = 64
AAA_LORA = 64
GATE_LORA = 128
LORA_WIDTH = DECAY_LORA + AAA_LORA + GATE_LORA
GN_EPS = HEAD_SIZE * 1e-5
RMS_EPS = 1e-6

CHUNK = 64
PAIR = 2 * HEAD_SIZE
GROUP = 4 * HEAD_SIZE
HALO = 128
HIST_ROWS = 16
SHIFT_PAD = 8
BF16_SUBLANES = 16

FFN_ROWS = 1024
FFN_SUB = 256
FFN_COLS = 256
MIX_ROWS = 512
SUB_ROWS = 256

C_RKV = 0
C_A2 = 3 * RWKV_WIDTH
C_SHIFT_END = C_A2 + LORA_WIDTH
C_POOL = C_SHIFT_END
C_GATE = C_POOL + POOL_WIDTH
C_A1 = C_GATE + 2 * D_MODEL
C_END = C_A1 + LORA_WIDTH

VMEM_LIMIT = 56 * 1024 * 1024


def _dot(a, b):
    return jnp.dot(a.astype(BF16), b.astype(BF16), preferred_element_type=F32)


def _dot_nt(a, b):
    return lax.dot_general(a.astype(BF16), b.astype(BF16), (((1,), (1,)), ((), ())),
                           preferred_element_type=F32)


def _dot_tn(a, b):
    return lax.dot_general(a.astype(BF16), b.astype(BF16), (((0,), (0,)), ((), ())),
                           preferred_element_type=F32)


def _rms(x, g):
    ms = jnp.mean(x * x, axis=-1, keepdims=True)
    return x * lax.rsqrt(ms + RMS_EPS) * g


def _ffn_body(*refs, n_cast):
    x_ref, gains_ref, wg_ref, wu_ref, wd_ref = refs[:5]
    cast_in = refs[5:5 + n_cast]
    o_ref = refs[5 + n_cast]
    cast_out = refs[6 + n_cast:]
    for src, dst in zip(cast_in, cast_out):
        dst[...] = src[...].astype(BF16)

    n_sub = x_ref.shape[0] // FFN_SUB
    d_ff = wg_ref.shape[1]

    def prologue(i):
        return _rms(x_ref[i * FFN_SUB:(i + 1) * FFN_SUB, :], gains_ref[0:1, :]).astype(BF16)

    def epilogue(i, f):
        rows = slice(i * FFN_SUB, (i + 1) * FFN_SUB)
        o_ref[rows, :] = x_ref[rows, :] + 0.5 * _rms(f, gains_ref[1:2, :])

    h = prologue(0)
    f_prev = None
    for i in range(n_sub):
        acts = []
        h_next = None
        for c0 in range(0, d_ff, FFN_COLS):
            gate = jnp.dot(h, wg_ref[:, c0:c0 + FFN_COLS], preferred_element_type=F32)
            up = jnp.dot(h, wu_ref[:, c0:c0 + FFN_COLS], preferred_element_type=F32)
            acts.append((gate * jax.nn.sigmoid(gate) * up).astype(BF16))
            if c0 == 0 and i + 1 < n_sub:
                h_next = prologue(i + 1)
            if c0 == 2 * FFN_COLS and f_prev is not None:
                epilogue(i - 1, f_prev)
        f_prev = jnp.dot(jnp.concatenate(acts, axis=1), wd_ref[...], preferred_element_type=F32)
        h = h_next
    epilogue(n_sub - 1, f_prev)


def _const_spec(shape):
    return pl.BlockSpec(shape, lambda *_: (0,) * len(shape), pipeline_mode=pl.Buffered(1))


def _cast_rows(rows, steps):
    for blk in range(-(-rows // steps), rows + 1):
        if blk % BF16_SUBLANES == 0 and rows % blk == 0:
            return blk
    raise ValueError(f"no bf16-aligned row block covers {rows} rows in {steps} steps")


def _ffn(x2d, gains, wg, wu, wd, cast=()):
    t, d = x2d.shape
    steps = t // FFN_ROWS
    cast_specs = []
    for w in cast:
        blk = _cast_rows(w.shape[0], steps)
        last = w.shape[0] // blk - 1
        cast_specs.append(pl.BlockSpec(
            (blk, w.shape[1]), functools.partial(lambda i, last: (jnp.minimum(i, last), 0), last=last)))
    out = pl.pallas_call(
        functools.partial(_ffn_body, n_cast=len(cast)),
        grid=(steps,),
        in_specs=[pl.BlockSpec((FFN_ROWS, d), lambda i: (i, 0))]
                 + [_const_spec(c.shape) for c in (gains, wg, wu, wd)] + cast_specs,
        out_specs=[pl.BlockSpec((FFN_ROWS, d), lambda i: (i, 0))] + cast_specs,
        out_shape=[jax.ShapeDtypeStruct((t, d), F32)]
                  + [jax.ShapeDtypeStruct(w.shape, BF16) for w in cast],
        compiler_params=pltpu.CompilerParams(
            dimension_semantics=("arbitrary",), vmem_limit_bytes=VMEM_LIMIT),
        name="macaron_ffn",
    )(x2d, gains, wg, wu, wd, *cast)
    return out


def _head_sum(x, ones_ref):
    half = ones_ref.shape[0]
    parts = [_dot(x[:, c:c + half], ones_ref[...]) for c in range(0, x.shape[1], half)]
    return jnp.concatenate(parts, axis=1)


def _wkv_masks():
    rowg = lax.broadcasted_iota(jnp.int32, (GROUP, GROUP), 0)
    colg = lax.broadcasted_iota(jnp.int32, (GROUP, GROUP), 1)
    row = lax.broadcasted_iota(jnp.int32, (2 * CHUNK, 2 * GROUP), 0)
    col = lax.broadcasted_iota(jnp.int32, (2 * CHUNK, 2 * GROUP), 1)
    lane = lax.broadcasted_iota(jnp.int32, (1, PAIR), 1)
    m0 = (lane < HEAD_SIZE).astype(F32)
    return dict(
        bd=(rowg // HEAD_SIZE == colg // HEAD_SIZE).astype(F32).astype(BF16),
        tri=(col % CHUNK) < (row % CHUNK) + jnp.where(row < CHUNK, 0, 1),
        eye=(lax.broadcasted_iota(jnp.int32, (CHUNK, GROUP), 0)
             == lax.broadcasted_iota(jnp.int32, (CHUNK, GROUP), 1) % CHUNK).astype(F32),
        m0=jnp.concatenate([m0, m0], axis=1),
        m1=jnp.concatenate([1.0 - m0, 1.0 - m0], axis=1))


def _bd(x, masks):
    xb = x.astype(BF16)
    return jnp.concatenate([xb] * (GROUP // HEAD_SIZE), axis=0) * masks["bd"]


def _wkv_parallel(chunks, masks, fill):
    items = range(len(chunks))
    bd = lambda x: _bd(x, masks)
    eye = masks["eye"]
    a_n = [c["at"] for c in chunks]
    r_n = [c["rt"] for c in chunks]
    b_n = [c["bt"] for c in chunks]
    k_n = [c["kt"] for c in chunks]
    v_n = [c["v"] for c in chunks]
    wc = [c["wc"] for c in chunks]

    o = [jnp.where(masks["tri"],
                   _dot(jnp.concatenate([a_n[i], r_n[i]], axis=0),
                        jnp.concatenate([bd(b_n[i]).T, bd(k_n[i]).T], axis=1)), 0.0)
         for i in items]
    fill()
    av = [_dot(o[i][:, GROUP:2 * GROUP], bd(v_n[i])) for i in items]
    fill()

    lk = [o[i][0:CHUNK, 0:GROUP] for i in items]
    arb = [o[i][CHUNK:2 * CHUNK, 0:GROUP] for i in items]
    t_inv = [eye + lk[i] for i in items]
    prod = [_dot(jnp.concatenate([lk[i], arb[i]], axis=0), bd(lk[i])) for i in items]
    lk = [prod[i][0:CHUNK] for i in items]
    arb_t = [arb[i] + prod[i][CHUNK:2 * CHUNK] for i in items]
    fill()
    n_sq = CHUNK.bit_length() - 2
    for _ in range(n_sq):
        prod = [_dot(jnp.concatenate([lk[i], t_inv[i], arb_t[i]], axis=0), bd(lk[i])) for i in items]
        lk = [prod[i][0:CHUNK] for i in items]
        t_inv = [t_inv[i] + prod[i][CHUNK:2 * CHUNK] for i in items]
        arb_t = [arb_t[i] + prod[i][2 * CHUNK:3 * CHUNK] for i in items]
        fill()
    prod = [_dot(jnp.concatenate([t_inv[i], arb_t[i]], axis=0), bd(lk[i])) for i in items]
    t_inv = [t_inv[i] + prod[i][0:CHUNK] for i in items]
    arb_t = [arb_t[i] + prod[i][CHUNK:2 * CHUNK] for i in items]
    fill()
    fill()

    pq = [_dot(jnp.concatenate([t_inv[i], arb_t[i]], axis=0),
               jnp.concatenate([bd(a_n[i]), bd(av[i][0:CHUNK])], axis=1)) for i in items]
    fill()

    out = []
    for i in items:
        bkw_t = jnp.concatenate([b_n[i] * wc[i], k_n[i] * wc[i]], axis=0).astype(BF16).T
        g_parts, h_parts = [], []
        for pp in range(GROUP // PAIR):
            ls = slice(pp * PAIR, (pp + 1) * PAIR)
            p_p = pq[i][0:CHUNK, ls]
            q_p = pq[i][0:CHUNK, GROUP + pp * PAIR:GROUP + (pp + 1) * PAIR]
            w_p = jnp.concatenate(
                [jnp.concatenate([p_p, q_p], axis=1),
                 jnp.concatenate([jnp.zeros_like(p_p), v_n[i][:, ls]], axis=1)], axis=0)
            gh = _dot(bkw_t[ls, :], w_p)
            sel = gh[0:HEAD_SIZE] * masks["m0"] + gh[HEAD_SIZE:PAIR] * masks["m1"]
            g_parts.append(sel[:, 0:PAIR])
            h_parts.append(sel[:, PAIR:2 * PAIR])
        out.append(dict(
            g=jnp.concatenate(g_parts, axis=1) + eye * wc[i],
            h=jnp.concatenate(h_parts, axis=1),
            ry=r_n[i] + pq[i][CHUNK:2 * CHUNK, 0:GROUP],
            yc=pq[i][CHUNK:2 * CHUNK, GROUP:2 * GROUP] + av[i][CHUNK:2 * CHUNK]))
    fill()
    return out


def _wkv_state_step(par, s_nat, masks):
    out = _dot(jnp.concatenate([par["g"], par["ry"]], axis=0), _bd(s_nat, masks))
    return out[CHUNK:2 * CHUNK] + par["yc"], out[0:CHUNK] + par["h"]


def _mix_body(x_ref, vd_ref, v5_ref, wcat_ref, wb_ref, wpool_ref, wbr_ref, wbp_ref, wout_ref,
              tri_ref, band_ref, bandh_ref, ones_ref, o_ref,
              sh_ref, pool_ref, y_ref, st_ref):
    tm = x_ref.shape[0]
    seq_tile = pl.program_id(1)

    @pl.when(seq_tile == 0)
    def _():
        sh_ref[0:SHIFT_PAD, :] = jnp.zeros((SHIFT_PAD, sh_ref.shape[1]), F32)
        pool_ref[0:HALO, :] = jnp.zeros((HALO, POOL_WIDTH), F32)
        st_ref[...] = jnp.zeros(st_ref.shape, F32)

    h = _rms(x_ref[...], vd_ref[0:1, :]).astype(BF16)

    sh_ref[SHIFT_PAD:SHIFT_PAD + tm, :] = jnp.dot(
        h, wcat_ref[:, C_RKV:C_SHIFT_END], preferred_element_type=F32)
    pool_ref[HALO:HALO + tm, :] = jnp.dot(
        h, wcat_ref[:, C_POOL:C_GATE], preferred_element_type=F32)
    a1 = jnp.dot(h, wcat_ref[:, C_A1:C_END], preferred_element_type=F32)

    gates = [None] * (2 * D_MODEL // GROUP)
    mixed = [None] * (tm // SUB_ROWS)
    pool_out = {}

    def gate_piece(i):
        def run():
            c0 = C_GATE + i * GROUP
            logits = jnp.dot(h, wcat_ref[:, c0:c0 + GROUP], preferred_element_type=F32)
            bias = vd_ref[2 + (i * GROUP) // D_MODEL:3 + (i * GROUP) // D_MODEL,
                          (i * GROUP) % D_MODEL:(i * GROUP) % D_MODEL + GROUP]
            gates[i] = jax.nn.sigmoid((logits + bias).astype(BF16))
        return run

    def band_piece(j):
        def run():
            r0 = j * SUB_ROWS
            pcur = pool_ref[HALO + r0:HALO + r0 + SUB_ROWS, :]
            phist = pool_ref[r0:r0 + HALO, :]
            t_abs = seq_tile * tm + r0 + lax.broadcasted_iota(jnp.int32, (SUB_ROWS, 1), 0)
            parts = []
            for gi, win in enumerate(POOL_WINDOWS):
                gs = slice(gi * POOL_GROUP, (gi + 1) * POOL_GROUP)
                wsum = jnp.dot(band_ref[gi], pcur[:, gs].astype(BF16), preferred_element_type=F32)
                whist = jnp.dot(bandh_ref[gi], phist[:, gs].astype(BF16), preferred_element_type=F32)
                wsum = jnp.concatenate([wsum[0:HIST_ROWS] + whist, wsum[HIST_ROWS:]], axis=0)
                inv_count = 1.0 / jnp.minimum(t_abs + 1, win).astype(F32)
                parts.append(wsum * inv_count - pcur[:, gs])
            mixed[j] = jnp.concatenate(parts, axis=1)
        return run

    def pool_proj():
        pool_ref[0:HALO, :] = pool_ref[tm:tm + HALO, :]
        pool_out["y"] = (_dot(jnp.concatenate(mixed, axis=0), wpool_ref[...])
                         * v5_ref[10:11, :]).astype(BF16)

    def pool_branch_piece(i):
        def run():
            half = D_MODEL // 2
            pool_out[i] = jnp.dot(pool_out["y"], wbp_ref[:, i * half:(i + 1) * half],
                                  preferred_element_type=F32).astype(BF16)
        return run

    fillers = ([band_piece(j) for j in range(tm // SUB_ROWS)] + [pool_proj]
               + [pool_branch_piece(0), pool_branch_piece(1)]
               + [gate_piece(i) for i in range(len(gates))])

    def fill(n=1):
        for _ in range(n):
            if fillers:
                fillers.pop(0)()

    rw = RWKV_WIDTH
    mu_r, mu_k, mu_v = v5_ref[0:1, :], v5_ref[1:2, :], v5_ref[2:3, :]
    w0, a0 = v5_ref[3:4, :], v5_ref[4:5, :]
    k_k, k_a, r_k = v5_ref[5:6, :], v5_ref[6:7, :], v5_ref[7:8, :]
    ln_w, ln_b = v5_ref[8:9, :], v5_ref[9:10, :]
    lane = lax.broadcasted_iota(jnp.int32, (1, LORA_WIDTH), 1)

    prep = []
    for r0 in range(0, tm, SUB_ROWS):
        cur = sh_ref[SHIFT_PAD + r0:SHIFT_PAD + r0 + SUB_ROWS, :]
        prv = sh_ref[SHIFT_PAD - 1 + r0:SHIFT_PAD - 1 + r0 + SUB_ROWS, :]

        def lerp(c0, mu):
            c = cur[:, c0:c0 + rw]
            return c + (prv[:, c0:c0 + rw] - c) * mu

        r = lerp(0, mu_r)
        k = lerp(rw, mu_k)
        v = lerp(2 * rw, mu_v)
        fill()

        lpre = a1[r0:r0 + SUB_ROWS] + prv[:, C_A2:C_SHIFT_END]
        lact = jnp.where(lane < DECAY_LORA, jnp.tanh(lpre),
                         jnp.where(lane < DECAY_LORA + AAA_LORA, lpre, jax.nn.sigmoid(lpre)))
        lora = _dot(lact, wb_ref[...])
        ld = -math.exp(-0.5) * jax.nn.sigmoid(w0 + lora[:, 0:rw])
        a = jax.nn.sigmoid(a0 + lora[:, rw:2 * rw])
        g = lora[:, 2 * rw:3 * rw]
        fill()

        kk = k * k_k
        kk = kk * lax.rsqrt(jnp.maximum(_head_sum(kk * kk, ones_ref), 1e-24))
        k2 = k * (1.0 + (a - 1.0) * k_a)
        bvec = kk * a
        fill()

        ld_hi = ld.astype(BF16)
        ld_lo = (ld - ld_hi.astype(F32)).astype(BF16)
        lc = (jnp.dot(tri_ref[...], ld_hi, preferred_element_type=F32)
              + jnp.dot(tri_ref[...], ld_lo, preferred_element_type=F32))
        e_w = jnp.exp(lc)
        e_wi = jnp.exp(-lc)
        fill()
        prep.append(dict(at=-kk * jnp.exp(lc - ld), rt=r * e_w, bt=bvec * e_wi, kt=k2 * e_wi,
                         v=v, e_w=e_w, g=g, bonus_in=r * k2 * r_k))
        fill()
    sh_ref[SHIFT_PAD - 1:SHIFT_PAD, :] = sh_ref[SHIFT_PAD - 1 + tm:SHIFT_PAD + tm, :]
    fill(len(fillers))

    masks = _wkv_masks()
    n_groups = rw // GROUP
    groups = [slice(q * GROUP, (q + 1) * GROUP) for q in range(n_groups)]
    state = [st_ref[:, gs] for gs in groups]

    def chunk_items(p):
        return [dict(at=p["at"][c0:c0 + CHUNK, gs], rt=p["rt"][c0:c0 + CHUNK, gs],
                     bt=p["bt"][c0:c0 + CHUNK, gs], kt=p["kt"][c0:c0 + CHUNK, gs],
                     v=p["v"][c0:c0 + CHUNK, gs], wc=p["e_w"][c0 + CHUNK - 1:c0 + CHUNK, gs])
                for c0 in range(0, SUB_ROWS, CHUNK) for gs in groups]

    def state_steps(wave, par):
        def step(ci):
            def run():
                for q, gs in enumerate(groups):
                    y, state[q] = _wkv_state_step(par[ci * n_groups + q], state[q], masks)
                    r0 = wave * SUB_ROWS + ci * CHUNK
                    y_ref[r0:r0 + CHUNK, gs] = y
            return run
        return [step(ci) for ci in range(SUB_ROWS // CHUNK)]

    n_gate = D_MODEL // GROUP
    gate0 = jnp.concatenate(gates[0:n_gate], axis=1)
    gate1 = jnp.concatenate(gates[n_gate:2 * n_gate], axis=1)
    y_pool = jnp.concatenate([pool_out[0], pool_out[1]], axis=1)

    def output_rows(wave):
        rows = slice(wave * SUB_ROWS, (wave + 1) * SUB_ROWS)
        p = prep[wave]
        y = y_ref[rows, :]
        inv_n = 1.0 / HEAD_SIZE
        mean = _head_sum(y, ones_ref) * inv_n
        yield
        yc = y - mean
        var = _head_sum(yc * yc, ones_ref) * inv_n
        yield
        yn = yc * lax.rsqrt(var + GN_EPS) * ln_w + ln_b
        bonus = _head_sum(p["bonus_in"], ones_ref) * p["v"]
        y_rwkv = (yn + bonus) * p["g"]
        yield
        merged = gate0[rows] * _dot(y_rwkv, wbr_ref[...]).astype(BF16) + gate1[rows] * y_pool[rows]
        yield
        o_ref[rows, :] = x_ref[rows, :] + _rms(_dot(merged, wout_ref[...]), vd_ref[1:2, :])

    pending = []

    def fill_pending():
        if pending:
            pending.pop(0)()

    n_waves = tm // SUB_ROWS
    for wave in range(n_waves):
        par = _wkv_parallel(chunk_items(prep[wave]), masks, fill_pending)
        while pending:
            fill_pending()
        pending = state_steps(wave, par)
        if wave > 0:
            for _ in output_rows(wave - 1):
                fill_pending()
    while pending:
        fill_pending()
    for q, gs in enumerate(groups):
        st_ref[:, gs] = state[q]
    for _ in output_rows(n_waves - 1):
        pass


def _mix(x2d, bsz, seq, vd, v5, wcat, wb, wpool, wbr, wbp, wout, tri, band, bandh, ones):
    tm = MIX_ROWS
    d = x2d.shape[1]
    tiles = seq // tm
    consts = (vd, v5, wcat, wb, wpool, wbr, wbp, wout, tri, band, bandh, ones)
    return pl.pallas_call(
        _mix_body,
        grid=(bsz, tiles),
        in_specs=[pl.BlockSpec((tm, d), lambda b, s: (b * tiles + s, 0))]
                 + [_const_spec(c.shape) for c in consts],
        out_specs=pl.BlockSpec((tm, d), lambda b, s: (b * tiles + s, 0)),
        out_shape=jax.ShapeDtypeStruct(x2d.shape, F32),
        scratch_shapes=[
            pltpu.VMEM((SHIFT_PAD + tm, C_SHIFT_END), F32),
            pltpu.VMEM((HALO + tm, POOL_WIDTH), F32),
            pltpu.VMEM((tm, RWKV_WIDTH), F32),
            pltpu.VMEM((HEAD_SIZE, RWKV_WIDTH), F32),
        ],
        compiler_params=pltpu.CompilerParams(
            dimension_semantics=("arbitrary", "arbitrary"), vmem_limit_bytes=VMEM_LIMIT),
        name="token_mixing",
    )(x2d, *consts)


def _mix_constants(tm):
    idx = np.arange(tm)
    tri = ((idx[:, None] // CHUNK == idx[None, :] // CHUNK) & (idx[None, :] <= idx[:, None]))
    band = np.stack([(idx[None, :] <= idx[:, None]) & (idx[None, :] > idx[:, None] - w)
                     for w in POOL_WINDOWS])
    hist = np.arange(HALO) - HALO
    first = np.arange(HIST_ROWS)
    bandh = np.stack([hist[None, :] > first[:, None] - w for w in POOL_WINDOWS])
    hid = np.arange(4 * HEAD_SIZE) // HEAD_SIZE
    ones = hid[:, None] == hid[None, :]
    as_bf16 = lambda m: jnp.asarray(m.astype(np.float32), dtype=BF16)
    return as_bf16(tri), as_bf16(band), as_bf16(bandh), as_bf16(ones)


def _block_diag(blocks):
    n = len(blocks)
    rows = []
    for i, blk in enumerate(blocks):
        rows.append(jnp.concatenate(
            [blk if j == i else jnp.zeros((blk.shape[0], blocks[j].shape[1]), blk.dtype)
             for j in range(n)], axis=1))
    return jnp.concatenate(rows, axis=0)


def kernel(x, norm_gains, ffn1_gate, ffn1_up, ffn1_down, w_in, gate_bias, mu_rkv, mu_wag, w0,
           decay_a, decay_b, a0, aaa_a, aaa_b, gate_a, gate_b, k_k, k_a, r_k, ln_x_w, ln_x_b,
           pool_w, pool_scale, w_branch_rwkv, w_branch_pool, w_out, ffn2_gate, ffn2_up, ffn2_down):
    bsz, seq, d = x.shape
    depth = norm_gains.shape[0]
    tri, band, bandh, ones = _mix_constants(SUB_ROWS)
    x2d = x.reshape(bsz * seq, d)
    for l in range(depth):
        g = norm_gains[l]
        x2d, wg2, wu2, wd2 = _ffn(x2d, g[0:2], ffn1_gate[l].astype(BF16), ffn1_up[l].astype(BF16),
                                  ffn1_down[l].astype(BF16),
                                  cast=(ffn2_gate[l], ffn2_up[l], ffn2_down[l]))

        lora_a = jnp.concatenate([decay_a[l], aaa_a[l], gate_a[l]], axis=1)
        mu = jnp.concatenate([jnp.broadcast_to(mu_wag[l, 0][:, None], decay_a[l].shape),
                              jnp.broadcast_to(mu_wag[l, 1][:, None], aaa_a[l].shape),
                              jnp.broadcast_to(mu_wag[l, 2][:, None], gate_a[l].shape)], axis=1)
        w_split = w_in[l]
        wcat = jnp.concatenate(
            [w_split[:, 0:3 * RWKV_WIDTH], mu * lora_a, w_split[:, 3 * RWKV_WIDTH:],
             (1.0 - mu) * lora_a], axis=1).astype(BF16)
        wb = _block_diag([decay_b[l], aaa_b[l], gate_b[l]]).astype(BF16)
        wpool = _block_diag([pool_w[l, i] for i in range(pool_w.shape[1])]).astype(BF16)
        vd = jnp.stack([g[2], g[3], gate_bias[l, 0], gate_bias[l, 1]])
        v5 = jnp.stack([mu_rkv[l, 0], mu_rkv[l, 1], mu_rkv[l, 2], w0[l], a0[l], k_k[l], k_a[l],
                        r_k[l].reshape(-1), ln_x_w[l], ln_x_b[l], pool_scale[l]])
        x2d = _mix(x2d, bsz, seq, vd, v5, wcat, wb, wpool,
                   w_branch_rwkv[l].astype(BF16), w_branch_pool[l].astype(BF16),
                   w_out[l].astype(BF16), tri, band, bandh, ones)

        x2d, = _ffn(x2d, g[4:6], wg2, wu2, wd2)
    return x2d.reshape(bsz, seq, d)
```

```python
import functools
import math

import numpy as np
import jax
import jax.numpy as jnp
from jax import lax
from jax.experimental import pallas as pl
from jax.experimental.pallas import tpu as pltpu

F32 = jnp.float32
BF16 = jnp.bfloat16

D_MODEL = 1024
RWKV_WIDTH = 512
HEAD_SIZE = 64
POOL_WIDTH = 512
POOL_WINDOWS = (2, 4, 8, 16)
POOL_GROUP = 128
DECAY_LORA = 64
AAA_LORA = 64
GATE_LORA = 128
LORA_WIDTH = DECAY_LORA + AAA_LORA + GATE_LORA
GN_EPS = HEAD_SIZE * 1e-5
RMS_EPS = 1e-6

CHUNK = 64
PAIR = 2 * HEAD_SIZE
GROUP = 4 * HEAD_SIZE
HALO = 128
HIST_ROWS = 16
SHIFT_PAD = 8
BF16_SUBLANES = 16

FFN_ROWS = 1024
FFN_SUB = 256
FFN_COLS = 256
MIX_ROWS = 512
SUB_ROWS = 256

C_RKV = 0
C_A2 = 3 * RWKV_WIDTH
C_SHIFT_END = C_A2 + LORA_WIDTH
C_POOL = C_SHIFT_END
C_GATE = C_POOL + POOL_WIDTH
C_A1 = C_GATE + 2 * D_MODEL
C_END = C_A1 + LORA_WIDTH

VMEM_LIMIT = 56 * 1024 * 1024


def _dot(a, b):
    return jnp.dot(a.astype(BF16), b.astype(BF16), preferred_element_type=F32)


def _rms(x, g):
    ms = jnp.mean(x * x, axis=-1, keepdims=True)
    return x * lax.rsqrt(ms + RMS_EPS) * g


def _ffn_body(*refs, n_cast):
    x_ref, gains_ref, wg_ref, wu_ref, wd_ref = refs[:5]
    cast_in = refs[5:5 + n_cast]
    o_ref = refs[5 + n_cast]
    cast_out = refs[6 + n_cast:]
    for src, dst in zip(cast_in, cast_out):
        dst[...] = src[...].astype(BF16)

    n_sub = x_ref.shape[0] // FFN_SUB
    d_ff = wg_ref.shape[1]

    def prologue(i):
        return _rms(x_ref[i * FFN_SUB:(i + 1) * FFN_SUB, :], gains_ref[0:1, :]).astype(BF16)

    def epilogue(i, f):
        rows = slice(i * FFN_SUB, (i + 1) * FFN_SUB)
        o_ref[rows, :] = x_ref[rows, :] + 0.5 * _rms(f, gains_ref[1:2, :])

    h = prologue(0)
    f_prev = None
    for i in range(n_sub):
        acts = []
        h_next = None
        for c0 in range(0, d_ff, FFN_COLS):
            gate = jnp.dot(h, wg_ref[:, c0:c0 + FFN_COLS], preferred_element_type=F32)
            up = jnp.dot(h, wu_ref[:, c0:c0 + FFN_COLS], preferred_element_type=F32)
            acts.append((gate * jax.nn.sigmoid(gate) * up).astype(BF16))
            if c0 == 0 and i + 1 < n_sub:
                h_next = prologue(i + 1)
            if c0 == 2 * FFN_COLS and f_prev is not None:
                epilogue(i - 1, f_prev)
        f_prev = jnp.dot(jnp.concatenate(acts, axis=1), wd_ref[...], preferred_element_type=F32)
        h = h_next
    epilogue(n_sub - 1, f_prev)


def _const_spec(shape):
    return pl.BlockSpec(shape, lambda *_: (0,) * len(shape), pipeline_mode=pl.Buffered(1))


def _cast_rows(rows, steps):
    for blk in range(-(-rows // steps), rows + 1):
        if blk % BF16_SUBLANES == 0 and rows % blk == 0:
            return blk
    raise ValueError(f"no bf16-aligned row block covers {rows} rows in {steps} steps")


def _ffn(x2d, gains, wg, wu, wd, cast=()):
    t, d = x2d.shape
    steps = t // FFN_ROWS
    cast_specs = []
    for w in cast:
        blk = _cast_rows(w.shape[0], steps)
        last = w.shape[0] // blk - 1
        cast_specs.append(pl.BlockSpec(
            (blk, w.shape[1]), functools.partial(lambda i, last: (jnp.minimum(i, last), 0), last=last)))
    out = pl.pallas_call(
        functools.partial(_ffn_body, n_cast=len(cast)),
        grid=(steps,),
        in_specs=[pl.BlockSpec((FFN_ROWS, d), lambda i: (i, 0))]
                 + [_const_spec(c.shape) for c in (gains, wg, wu, wd)] + cast_specs,
        out_specs=[pl.BlockSpec((FFN_ROWS, d), lambda i: (i, 0))] + cast_specs,
        out_shape=[jax.ShapeDtypeStruct((t, d), F32)]
                  + [jax.ShapeDtypeStruct(w.shape, BF16) for w in cast],
        compiler_params=pltpu.CompilerParams(
            dimension_semantics=("arbitrary",), vmem_limit_bytes=VMEM_LIMIT),
        name="macaron_ffn",
    )(x2d, gains, wg, wu, wd, *cast)
    return out


def _head_sum(x, ones_ref):
    half = ones_ref.shape[0]
    parts = [_dot(x[:, c:c + half], ones_ref[...]) for c in range(0, x.shape[1], half)]
    return jnp.concatenate(parts, axis=1)


def _wkv_masks():
    rowg = lax.broadcasted_iota(jnp.int32, (GROUP, GROUP), 0)
    colg = lax.broadcasted_iota(jnp.int32, (GROUP, GROUP), 1)
    row = lax.broadcasted_iota(jnp.int32, (2 * CHUNK, 2 * GROUP), 0)
    col = lax.broadcasted_iota(jnp.int32, (2 * CHUNK, 2 * GROUP), 1)
    lane = lax.broadcasted_iota(jnp.int32, (1, PAIR), 1)
    m0 = (lane < HEAD_SIZE).astype(F32)
    return dict(
        bd=(rowg // HEAD_SIZE == colg // HEAD_SIZE).astype(F32).astype(BF16),
        tri=(col % CHUNK) < (row % CHUNK) + jnp.where(row < CHUNK, 0, 1),
        eye=(lax.broadcasted_iota(jnp.int32, (CHUNK, GROUP), 0)
             == lax.broadcasted_iota(jnp.int32, (CHUNK, GROUP), 1) % CHUNK).astype(F32),
        m0=jnp.concatenate([m0, m0], axis=1),
        m1=jnp.concatenate([1.0 - m0, 1.0 - m0], axis=1))


def _bd(x, masks):
    xb = x.astype(BF16)
    return jnp.concatenate([xb] * (GROUP // HEAD_SIZE), axis=0) * masks["bd"]


def _wkv_parallel(chunks, masks, fill):
    items = range(len(chunks))
    bd = lambda x: _bd(x, masks)
    eye = masks["eye"]
    a_n = [c["at"] for c in chunks]
    r_n = [c["rt"] for c in chunks]
    b_n = [c["bt"] for c in chunks]
    k_n = [c["kt"] for c in chunks]
    v_n = [c["v"] for c in chunks]
    wc = [c["wc"] for c in chunks]

    o = [jnp.where(masks["tri"],
                   _dot(jnp.concatenate([a_n[i], r_n[i]], axis=0),
                        jnp.concatenate([bd(b_n[i]).T, bd(k_n[i]).T], axis=1)), 0.0)
         for i in items]
    fill()
    av = [_dot(o[i][:, GROUP:2 * GROUP], bd(v_n[i])) for i in items]
    fill()

    lk = [o[i][0:CHUNK, 0:GROUP] for i in items]
    arb = [o[i][CHUNK:2 * CHUNK, 0:GROUP] for i in items]
    t_inv = [eye + lk[i] for i in items]
    prod = [_dot(jnp.concatenate([lk[i], arb[i]], axis=0), bd(lk[i])) for i in items]
    lk = [prod[i][0:CHUNK] for i in items]
    arb_t = [arb[i] + prod[i][CHUNK:2 * CHUNK] for i in items]
    fill()
    n_sq = CHUNK.bit_length() - 2
    for _ in range(n_sq):
        prod = [_dot(jnp.concatenate([lk[i], t_inv[i], arb_t[i]], axis=0), bd(lk[i])) for i in items]
        lk = [prod[i][0:CHUNK] for i in items]
        t_inv = [t_inv[i] + prod[i][CHUNK:2 * CHUNK] for i in items]
        arb_t = [arb_t[i] + prod[i][2 * CHUNK:3 * CHUNK] for i in items]
        fill()
    prod = [_dot(jnp.concatenate([t_inv[i], arb_t[i]], axis=0), bd(lk[i])) for i in items]
    t_inv = [t_inv[i] + prod[i][0:CHUNK] for i in items]
    arb_t = [arb_t[i] + prod[i][CHUNK:2 * CHUNK] for i in items]
    fill()

    pq = [_dot(jnp.concatenate([t_inv[i], arb_t[i]], axis=0),
               jnp.concatenate([bd(a_n[i]), bd(av[i][0:CHUNK])], axis=1)) for i in items]
    fill()

    out = []
    for i in items:
        bkw_t = jnp.concatenate([b_n[i] * wc[i], k_n[i] * wc[i]], axis=0).astype(BF16).T
        g_parts, h_parts = [], []
        for pp in range(GROUP // PAIR):
            ls = slice(pp * PAIR, (pp + 1) * PAIR)
            p_p = pq[i][0:CHUNK, ls]
            q_p = pq[i][0:CHUNK, GROUP + pp * PAIR:GROUP + (pp + 1) * PAIR]
            w_p = jnp.concatenate(
                [jnp.concatenate([p_p, q_p], axis=1),
                 jnp.concatenate([jnp.zeros_like(p_p), v_n[i][:, ls]], axis=1)], axis=0)
            gh = _dot(bkw_t[ls, :], w_p)
            sel = gh[0:HEAD_SIZE] * masks["m0"] + gh[HEAD_SIZE:PAIR] * masks["m1"]
            g_parts.append(sel[:, 0:PAIR])
            h_parts.append(sel[:, PAIR:2 * PAIR])
        out.append(dict(
            g=jnp.concatenate(g_parts, axis=1) + eye * wc[i],
            h=jnp.concatenate(h_parts, axis=1),
            ry=r_n[i] + pq[i][CHUNK:2 * CHUNK, 0:GROUP],
            yc=pq[i][CHUNK:2 * CHUNK, GROUP:2 * GROUP] + av[i][CHUNK:2 * CHUNK]))
    fill()
    return out


def _wkv_state_step(par, s_nat, masks):
    out = _dot(jnp.concatenate([par["g"], par["ry"]], axis=0), _bd(s_nat, masks))
    return out[CHUNK:2 * CHUNK] + par["yc"], out[0:CHUNK] + par["h"]


def _mix_body(x_ref, vd_ref, v5_ref, wcat_ref, wb_ref, wpool_ref, wbr_ref, wbp_ref, wout_ref,
              tri_ref, band_ref, bandh_ref, ones_ref, o_ref,
              sh_ref, pool_ref, y_ref, st_ref):
    tm = x_ref.shape[0]
    seq_tile = pl.program_id(1)

    @pl.when(seq_tile == 0)
    def _():
        sh_ref[0:SHIFT_PAD, :] = jnp.zeros((SHIFT_PAD, sh_ref.shape[1]), F32)
        pool_ref[0:HALO, :] = jnp.zeros((HALO, POOL_WIDTH), F32)
        st_ref[...] = jnp.zeros(st_ref.shape, F32)

    h = _rms(x_ref[...], vd_ref[0:1, :]).astype(BF16)

    sh_ref[SHIFT_PAD:SHIFT_PAD + tm, :] = jnp.dot(
        h, wcat_ref[:, C_RKV:C_SHIFT_END], preferred_element_type=F32)
    pool_ref[HALO:HALO + tm, :] = jnp.dot(
        h, wcat_ref[:, C_POOL:C_GATE], preferred_element_type=F32)
    a1 = jnp.dot(h, wcat_ref[:, C_A1:C_END], preferred_element_type=F32)

    gates = [None] * (2 * D_MODEL // GROUP)
    mixed = [None] * (tm // SUB_ROWS)
    pool_out = {}

    def gate_piece(i):
        def run():
            c0 = C_GATE + i * GROUP
            logits = jnp.dot(h, wcat_ref[:, c0:c0 + GROUP], preferred_element_type=F32)
            bias = vd_ref[2 + (i * GROUP) // D_MODEL:3 + (i * GROUP) // D_MODEL,
                          (i * GROUP) % D_MODEL:(i * GROUP) % D_MODEL + GROUP]
            gates[i] = jax.nn.sigmoid((logits + bias).astype(BF16))
        return run

    def band_piece(j):
        def run():
            r0 = j * SUB_ROWS
            pcur = pool_ref[HALO + r0:HALO + r0 + SUB_ROWS, :]
            phist = pool_ref[r0:r0 + HALO, :]
            t_abs = seq_tile * tm + r0 + lax.broadcasted_iota(jnp.int32, (SUB_ROWS, 1), 0)
            parts = []
            for gi, win in enumerate(POOL_WINDOWS):
                gs = slice(gi * POOL_GROUP, (gi + 1) * POOL_GROUP)
                wsum = jnp.dot(band_ref[gi], pcur[:, gs].astype(BF16), preferred_element_type=F32)
                whist = jnp.dot(bandh_ref[gi], phist[:, gs].astype(BF16), preferred_element_type=F32)
                wsum = jnp.concatenate([wsum[0:HIST_ROWS] + whist, wsum[HIST_ROWS:]], axis=0)
                inv_count = 1.0 / jnp.minimum(t_abs + 1, win).astype(F32)
                parts.append(wsum * inv_count - pcur[:, gs])
            mixed[j] = jnp.concatenate(parts, axis=1)
        return run

    def pool_proj():
        pool_ref[0:HALO, :] = pool_ref[tm:tm + HALO, :]
        pool_out["y"] = (_dot(jnp.concatenate(mixed, axis=0), wpool_ref[...])
                         * v5_ref[10:11, :]).astype(BF16)

    def pool_branch_piece(i):
        def run():
            half = D_MODEL // 2
            pool_out[i] = jnp.dot(pool_out["y"], wbp_ref[:, i * half:(i + 1) * half],
                                  preferred_element_type=F32).astype(BF16)
        return run

    fillers = ([band_piece(j) for j in range(tm // SUB_ROWS)] + [pool_proj]
               + [pool_branch_piece(0), pool_branch_piece(1)]
               + [gate_piece(i) for i in range(len(gates))])

    def fill(n=1):
        for _ in range(n):
            if fillers:
                fillers.pop(0)()

    rw = RWKV_WIDTH
    mu_r, mu_k, mu_v = v5_ref[0:1, :], v5_ref[1:2, :], v5_ref[2:3, :]
    w0, a0 = v5_ref[3:4, :], v5_ref[4:5, :]
    k_k, k_a, r_k = v5_ref[5:6, :], v5_ref[6:7, :], v5_ref[7:8, :]
    ln_w, ln_b = v5_ref[8:9, :], v5_ref[9:10, :]
    lane = lax.broadcasted_iota(jnp.int32, (1, LORA_WIDTH), 1)

    prep = []
    for r0 in range(0, tm, SUB_ROWS):
        cur = sh_ref[SHIFT_PAD + r0:SHIFT_PAD + r0 + SUB_ROWS, :]
        prv = sh_ref[SHIFT_PAD - 1 + r0:SHIFT_PAD - 1 + r0 + SUB_ROWS, :]

        def lerp(c0, mu):
            c = cur[:, c0:c0 + rw]
            return c + (prv[:, c0:c0 + rw] - c) * mu

        r = lerp(0, mu_r)
        k = lerp(rw, mu_k)
        v = lerp(2 * rw, mu_v)
        fill()

        lpre = a1[r0:r0 + SUB_ROWS] + prv[:, C_A2:C_SHIFT_END]
        lact = jnp.where(lane < DECAY_LORA, jnp.tanh(lpre),
                         jnp.where(lane < DECAY_LORA + AAA_LORA, lpre, jax.nn.sigmoid(lpre)))
        lora = _dot(lact, wb_ref[...])
        ld = -math.exp(-0.5) * jax.nn.sigmoid(w0 + lora[:, 0:rw])
        a = jax.nn.sigmoid(a0 + lora[:, rw:2 * rw])
        g = lora[:, 2 * rw:3 * rw]
        fill()

        kk = k * k_k
        kk = kk * lax.rsqrt(jnp.maximum(_head_sum(kk * kk, ones_ref), 1e-24))
        k2 = k * (1.0 + (a - 1.0) * k_a)
        bvec = kk * a
        fill()

        ld_hi = ld.astype(BF16)
        ld_lo = (ld - ld_hi.astype(F32)).astype(BF16)
        lc = (jnp.dot(tri_ref[...], ld_hi, preferred_element_type=F32)
              + jnp.dot(tri_ref[...], ld_lo, preferred_element_type=F32))
        e_w = jnp.exp(lc)
        e_wi = jnp.exp(-lc)
        fill()
        prep.append(dict(at=-kk * jnp.exp(lc - ld), rt=r * e_w, bt=bvec * e_wi, kt=k2 * e_wi,
                         v=v, e_w=e_w, g=g, bonus_in=r * k2 * r_k))
        fill()
    sh_ref[SHIFT_PAD - 1:SHIFT_PAD, :] = sh_ref[SHIFT_PAD - 1 + tm:SHIFT_PAD + tm, :]
    fill(len(fillers))

    masks = _wkv_masks()
    n_groups = rw // GROUP
    groups = [slice(q * GROUP, (q + 1) * GROUP) for q in range(n_groups)]
    state = [st_ref[:, gs] for gs in groups]

    def chunk_items(p):
        return [dict(at=p["at"][c0:c0 + CHUNK, gs], rt=p["rt"][c0:c0 + CHUNK, gs],
                     bt=p["bt"][c0:c0 + CHUNK, gs], kt=p["kt"][c0:c0 + CHUNK, gs],
                     v=p["v"][c0:c0 + CHUNK, gs], wc=p["e_w"][c0 + CHUNK - 1:c0 + CHUNK, gs])
                for c0 in range(0, SUB_ROWS, CHUNK) for gs in groups]

    def state_steps(wave, par):
        def step(ci):
            def run():
                for q, gs in enumerate(groups):
                    y, state[q] = _wkv_state_step(par[ci * n_groups + q], state[q], masks)
                    r0 = wave * SUB_ROWS + ci * CHUNK
                    y_ref[r0:r0 + CHUNK, gs] = y
            return run
        return [step(ci) for ci in range(SUB_ROWS // CHUNK)]

    n_gate = D_MODEL // GROUP
    gate0 = jnp.concatenate(gates[0:n_gate], axis=1)
    gate1 = jnp.concatenate(gates[n_gate:2 * n_gate], axis=1)
    y_pool = jnp.concatenate([pool_out[0], pool_out[1]], axis=1)

    def output_rows(wave):
        rows = slice(wave * SUB_ROWS, (wave + 1) * SUB_ROWS)
        p = prep[wave]
        y = y_ref[rows, :]
        inv_n = 1.0 / HEAD_SIZE
        mean = _head_sum(y, ones_ref) * inv_n
        yield
        yc = y - mean
        var = _head_sum(yc * yc, ones_ref) * inv_n
        yield
        yn = yc * lax.rsqrt(var + GN_EPS) * ln_w + ln_b
        bonus = _head_sum(p["bonus_in"], ones_ref) * p["v"]
        y_rwkv = (yn + bonus) * p["g"]
        yield
        merged = gate0[rows] * _dot(y_rwkv, wbr_ref[...]).astype(BF16) + gate1[rows] * y_pool[rows]
        yield
        o_ref[rows, :] = x_ref[rows, :] + _rms(_dot(merged, wout_ref[...]), vd_ref[1:2, :])

    pending = []

    def fill_pending():
        if pending:
            pending.pop(0)()

    n_waves = tm // SUB_ROWS
    for wave in range(n_waves):
        par = _wkv_parallel(chunk_items(prep[wave]), masks, fill_pending)
        while pending:
            fill_pending()
        pending = state_steps(wave, par)
        if wave > 0:
            for _ in output_rows(wave - 1):
                fill_pending()
    while pending:
        fill_pending()
    for q, gs in enumerate(groups):
        st_ref[:, gs] = state[q]
    for _ in output_rows(n_waves - 1):
        pass


def _mix(x2d, bsz, seq, vd, v5, wcat, wb, wpool, wbr, wbp, wout, tri, band, bandh, ones):
    tm = MIX_ROWS
    d = x2d.shape[1]
    tiles = seq // tm
    consts = (vd, v5, wcat, wb, wpool, wbr, wbp, wout, tri, band, bandh, ones)
    return pl.pallas_call(
        _mix_body,
        grid=(bsz, tiles),
        in_specs=[pl.BlockSpec((tm, d), lambda b, s: (b * tiles + s, 0))]
                 + [_const_spec(c.shape) for c in consts],
        out_specs=pl.BlockSpec((tm, d), lambda b, s: (b * tiles + s, 0)),
        out_shape=jax.ShapeDtypeStruct(x2d.shape, F32),
        scratch_shapes=[
            pltpu.VMEM((SHIFT_PAD + tm, C_SHIFT_END), F32),
            pltpu.VMEM((HALO + tm, POOL_WIDTH), F32),
            pltpu.VMEM((tm, RWKV_WIDTH), F32),
            pltpu.VMEM((HEAD_SIZE, RWKV_WIDTH), F32),
        ],
        compiler_params=pltpu.CompilerParams(
            dimension_semantics=("arbitrary", "arbitrary"), vmem_limit_bytes=VMEM_LIMIT),
        name="token_mixing",
    )(x2d, *consts)


def _mix_constants(tm):
    idx = np.arange(tm)
    tri = ((idx[:, None] // CHUNK == idx[None, :] // CHUNK) & (idx[None, :] <= idx[:, None]))
    band = np.stack([(idx[None, :] <= idx[:, None]) & (idx[None, :] > idx[:, None] - w)
                     for w in POOL_WINDOWS])
    hist = np.arange(HALO) - HALO
    first = np.arange(HIST_ROWS)
    bandh = np.stack([hist[None, :] > first[:, None] - w for w in POOL_WINDOWS])
    hid = np.arange(4 * HEAD_SIZE) // HEAD_SIZE
    ones = hid[:, None] == hid[None, :]
    as_bf16 = lambda m: jnp.asarray(m.astype(np.float32), dtype=BF16)
    return as_bf16(tri), as_bf16(band), as_bf16(bandh), as_bf16(ones)


def _block_diag(blocks):
    n = len(blocks)
    rows = []
    for i, blk in enumerate(blocks):
        rows.append(jnp.concatenate(
            [blk if j == i else jnp.zeros((blk.shape[0], blocks[j].shape[1]), blk.dtype)
             for j in range(n)], axis=1))
    return jnp.concatenate(rows, axis=0)


def kernel(x, norm_gains, ffn1_gate, ffn1_up, ffn1_down, w_in, gate_bias, mu_rkv, mu_wag, w0,
           decay_a, decay_b, a0, aaa_a, aaa_b, gate_a, gate_b, k_k, k_a, r_k, ln_x_w, ln_x_b,
           pool_w, pool_scale, w_branch_rwkv, w_branch_pool, w_out, ffn2_gate, ffn2_up, ffn2_down):
    bsz, seq, d = x.shape
    depth = norm_gains.shape[0]
    tri, band, bandh, ones = _mix_constants(SUB_ROWS)
    x2d = x.reshape(bsz * seq, d)
    for l in range(depth):
        g = norm_gains[l]
        x2d, wg2, wu2, wd2 = _ffn(x2d, g[0:2], ffn1_gate[l].astype(BF16), ffn1_up[l].astype(BF16),
                                  ffn1_down[l].astype(BF16),
                                  cast=(ffn2_gate[l], ffn2_up[l], ffn2_down[l]))

        lora_a = jnp.concatenate([decay_a[l], aaa_a[l], gate_a[l]], axis=1)
        mu = jnp.concatenate([jnp.broadcast_to(mu_wag[l, 0][:, None], decay_a[l].shape),
                              jnp.broadcast_to(mu_wag[l, 1][:, None], aaa_a[l].shape),
                              jnp.broadcast_to(mu_wag[l, 2][:, None], gate_a[l].shape)], axis=1)
        w_split = w_in[l]
        wcat = jnp.concatenate(
            [w_split[:, 0:3 * RWKV_WIDTH], mu * lora_a, w_split[:, 3 * RWKV_WIDTH:],
             (1.0 - mu) * lora_a], axis=1).astype(BF16)
        wb = _block_diag([decay_b[l], aaa_b[l], gate_b[l]]).astype(BF16)
        wpool = _block_diag([pool_w[l, i] for i in range(pool_w.shape[1])]).astype(BF16)
        vd = jnp.stack([g[2], g[3], gate_bias[l, 0], gate_bias[l, 1]])
        v5 = jnp.stack([mu_rkv[l, 0], mu_rkv[l, 1], mu_rkv[l, 2], w0[l], a0[l], k_k[l], k_a[l],
                        r_k[l].reshape(-1), ln_x_w[l], ln_x_b[l], pool_scale[l]])
        x2d = _mix(x2d, bsz, seq, vd, v5, wcat, wb, wpool,
                   w_branch_rwkv[l].astype(BF16), w_branch_pool[l].astype(BF16),
                   w_out[l].astype(BF16), tri, band, bandh, ones)

        x2d, = _ffn(x2d, g[4:6], wg2, wu2, wd2)
    return x2d.reshape(bsz, seq, d)
```

```python
import functools
import math

import numpy as np
import jax
import jax.numpy as jnp
from jax import lax
from jax.experimental import pallas as pl
from jax.experimental.pallas import tpu as pltpu

F32 = jnp.float32
BF16 = jnp.bfloat16

D_MODEL = 1024
RWKV_WIDTH = 512
HEAD_SIZE = 64
POOL_WIDTH = 512
POOL_WINDOWS = (2, 4, 8, 16)
POOL_GROUP = 128
DECAY_LORA = 64
AAA_LORA = 64
GATE_LORA = 128
LORA_WIDTH = DECAY_LORA + AAA_LORA + GATE_LORA
GN_EPS = HEAD_SIZE * 1e-5
RMS_EPS = 1e-6

CHUNK = 64
PAIR = 2 * HEAD_SIZE
GROUP = 4 * HEAD_SIZE
HALO = 128
HIST_ROWS = 16
SHIFT_PAD = 8
BF16_SUBLANES = 16

FFN_ROWS = 1024
FFN_SUB = 256
FFN_COLS = 256
MIX_ROWS = 512
SUB_ROWS = 256

C_RKV = 0
C_A2 = 3 * RWKV_WIDTH
C_SHIFT_END = C_A2 + LORA_WIDTH
C_POOL = C_SHIFT_END
C_GATE = C_POOL + POOL_WIDTH
C_A1 = C_GATE + 2 * D_MODEL
C_END = C_A1 + LORA_WIDTH

VMEM_LIMIT = 56 * 1024 * 1024


def _dot(a, b):
    return jnp.dot(a.astype(BF16), b.astype(BF16), preferred_element_type=F32)


def _rms(x, g):
    ms = jnp.mean(x * x, axis=-1, keepdims=True)
    return x * lax.rsqrt(ms + RMS_EPS) * g


def _ffn_body(*refs, n_cast):
    x_ref, gains_ref, wg_ref, wu_ref, wd_ref = refs[:5]
    cast_in = refs[5:5 + n_cast]
    o_ref = refs[5 + n_cast]
    cast_out = refs[6 + n_cast:]
    for src, dst in zip(cast_in, cast_out):
        dst[...] = src[...].astype(BF16)

    n_sub = x_ref.shape[0] // FFN_SUB
    d_ff = wg_ref.shape[1]

    def prologue(i):
        return _rms(x_ref[i * FFN_SUB:(i + 1) * FFN_SUB, :], gains_ref[0:1, :]).astype(BF16)

    def epilogue(i, f):
        rows = slice(i * FFN_SUB, (i + 1) * FFN_SUB)
        o_ref[rows, :] = x_ref[rows, :] + 0.5 * _rms(f, gains_ref[1:2, :])

    h = prologue(0)
    f_prev = None
    for i in range(n_sub):
        acts = []
        h_next = None
        for c0 in range(0, d_ff, FFN_COLS):
            gate = jnp.dot(h, wg_ref[:, c0:c0 + FFN_COLS], preferred_element_type=F32)
            up = jnp.dot(h, wu_ref[:, c0:c0 + FFN_COLS], preferred_element_type=F32)
            acts.append((gate * jax.nn.sigmoid(gate) * up).astype(BF16))
            if c0 == 0 and i + 1 < n_sub:
                h_next = prologue(i + 1)
            if c0 == 2 * FFN_COLS and f_prev is not None:
                epilogue(i - 1, f_prev)
        f_prev = jnp.dot(jnp.concatenate(acts, axis=1), wd_ref[...], preferred_element_type=F32)
        h = h_next
    epilogue(n_sub - 1, f_prev)


def _const_spec(shape):
    return pl.BlockSpec(shape, lambda *_: (0,) * len(shape), pipeline_mode=pl.Buffered(1))


def _cast_rows(rows, steps):
    for blk in range(-(-rows // steps), rows + 1):
        if blk % BF16_SUBLANES == 0 and rows % blk == 0:
            return blk
    raise ValueError(f"no bf16-aligned row block covers {rows} rows in {steps} steps")


def _ffn(x2d, gains, wg, wu, wd, cast=()):
    t, d = x2d.shape
    steps = t // FFN_ROWS
    cast_specs = []
    for w in cast:
        blk = _cast_rows(w.shape[0], steps)
        last = w.shape[0] // blk - 1
        cast_specs.append(pl.BlockSpec(
            (blk, w.shape[1]), functools.partial(lambda i, last: (jnp.minimum(i, last), 0), last=last)))
    out = pl.pallas_call(
        functools.partial(_ffn_body, n_cast=len(cast)),
        grid=(steps,),
        in_specs=[pl.BlockSpec((FFN_ROWS, d), lambda i: (i, 0))]
                 + [_const_spec(c.shape) for c in (gains, wg, wu, wd)] + cast_specs,
        out_specs=[pl.BlockSpec((FFN_ROWS, d), lambda i: (i, 0))] + cast_specs,
        out_shape=[jax.ShapeDtypeStruct((t, d), F32)]
                  + [jax.ShapeDtypeStruct(w.shape, BF16) for w in cast],
        compiler_params=pltpu.CompilerParams(
            dimension_semantics=("arbitrary",), vmem_limit_bytes=VMEM_LIMIT),
        name="macaron_ffn",
    )(x2d, gains, wg, wu, wd, *cast)
    return out


def _head_sum(x, ones_ref):
    half = ones_ref.shape[0]
    parts = [_dot(x[:, c:c + half], ones_ref[...]) for c in range(0, x.shape[1], half)]
    return jnp.concatenate(parts, axis=1)


def _wkv_masks():
    row = lax.broadcasted_iota(jnp.int32, (2 * CHUNK, 2 * GROUP), 0)
    col = lax.broadcasted_iota(jnp.int32, (2 * CHUNK, 2 * GROUP), 1)
    lane = lax.broadcasted_iota(jnp.int32, (1, PAIR), 1)
    m0 = (lane < HEAD_SIZE).astype(F32)
    return dict(
        lo=m0.astype(BF16), hi=(1.0 - m0).astype(BF16),
        tri=(col % CHUNK) < (row % CHUNK) + jnp.where(row < CHUNK, 0, 1),
        eye=(lax.broadcasted_iota(jnp.int32, (CHUNK, GROUP), 0)
             == lax.broadcasted_iota(jnp.int32, (CHUNK, GROUP), 1) % CHUNK).astype(F32),
        m0=jnp.concatenate([m0, m0], axis=1),
        m1=jnp.concatenate([1.0 - m0, 1.0 - m0], axis=1))


def _bd(x, masks):
    xb = x.astype(BF16)
    zero = jnp.zeros((x.shape[0], PAIR), BF16)
    blocks = []
    for p in range(GROUP // PAIR):
        tile = xb[:, p * PAIR:(p + 1) * PAIR]
        for m in (masks["lo"], masks["hi"]):
            row = [zero] * (GROUP // PAIR)
            row[p] = tile * m
            blocks.append(jnp.concatenate(row, axis=1))
    return jnp.concatenate(blocks, axis=0)


def _wkv_parallel(chunks, masks, fill):
    items = range(len(chunks))
    bd = lambda x: _bd(x, masks)
    eye = masks["eye"]
    a_n = [c["at"] for c in chunks]
    r_n = [c["rt"] for c in chunks]
    b_n = [c["bt"] for c in chunks]
    k_n = [c["kt"] for c in chunks]
    v_n = [c["v"] for c in chunks]
    wc = [c["wc"] for c in chunks]

    o = [jnp.where(masks["tri"],
                   _dot(jnp.concatenate([a_n[i], r_n[i]], axis=0),
                        jnp.concatenate([bd(b_n[i]).T, bd(k_n[i]).T], axis=1)), 0.0)
         for i in items]
    fill()
    av = [_dot(o[i][:, GROUP:2 * GROUP], bd(v_n[i])) for i in items]
    fill()

    lk = [o[i][0:CHUNK, 0:GROUP] for i in items]
    arb = [o[i][CHUNK:2 * CHUNK, 0:GROUP] for i in items]
    t_inv = [eye + lk[i] for i in items]
    prod = [_dot(jnp.concatenate([lk[i], arb[i]], axis=0), bd(lk[i])) for i in items]
    lk = [prod[i][0:CHUNK] for i in items]
    arb_t = [arb[i] + prod[i][CHUNK:2 * CHUNK] for i in items]
    fill()
    n_sq = CHUNK.bit_length() - 2
    for _ in range(n_sq):
        prod = [_dot(jnp.concatenate([lk[i], t_inv[i], arb_t[i]], axis=0), bd(lk[i])) for i in items]
        lk = [prod[i][0:CHUNK] for i in items]
        t_inv = [t_inv[i] + prod[i][CHUNK:2 * CHUNK] for i in items]
        arb_t = [arb_t[i] + prod[i][2 * CHUNK:3 * CHUNK] for i in items]
        fill()
    prod = [_dot(jnp.concatenate([t_inv[i], arb_t[i]], axis=0), bd(lk[i])) for i in items]
    t_inv = [t_inv[i] + prod[i][0:CHUNK] for i in items]
    arb_t = [arb_t[i] + prod[i][CHUNK:2 * CHUNK] for i in items]
    fill()

    pq = [_dot(jnp.concatenate([t_inv[i], arb_t[i]], axis=0),
               jnp.concatenate([bd(a_n[i]), bd(av[i][0:CHUNK])], axis=1)) for i in items]
    fill()

    out = []
    for i in items:
        bkw_t = jnp.concatenate([b_n[i] * wc[i], k_n[i] * wc[i]], axis=0).astype(BF16).T
        g_parts, h_parts = [], []
        for pp in range(GROUP // PAIR):
            ls = slice(pp * PAIR, (pp + 1) * PAIR)
            p_p = pq[i][0:CHUNK, ls]
            q_p = pq[i][0:CHUNK, GROUP + pp * PAIR:GROUP + (pp + 1) * PAIR]
            w_p = jnp.concatenate(
                [jnp.concatenate([p_p, q_p], axis=1),
                 jnp.concatenate([jnp.zeros_like(p_p), v_n[i][:, ls]], axis=1)], axis=0)
            gh = _dot(bkw_t[ls, :], w_p)
            sel = gh[0:HEAD_SIZE] * masks["m0"] + gh[HEAD_SIZE:PAIR] * masks["m1"]
            g_parts.append(sel[:, 0:PAIR])
            h_parts.append(sel[:, PAIR:2 * PAIR])
        out.append(dict(
            g=jnp.concatenate(g_parts, axis=1) + eye * wc[i],
            h=jnp.concatenate(h_parts, axis=1),
            ry=r_n[i] + pq[i][CHUNK:2 * CHUNK, 0:GROUP],
            yc=pq[i][CHUNK:2 * CHUNK, GROUP:2 * GROUP] + av[i][CHUNK:2 * CHUNK]))
    fill()
    return out


def _wkv_state_step(par, s_nat, masks):
    out = _dot(jnp.concatenate([par["g"], par["ry"]], axis=0), _bd(s_nat, masks))
    return out[CHUNK:2 * CHUNK] + par["yc"], out[0:CHUNK] + par["h"]


def _mix_body(x_ref, vd_ref, v5_ref, wcat_ref, wb_ref, wpool_ref, wbr_ref, wbp_ref, wout_ref,
              tri_ref, band_ref, bandh_ref, ones_ref, o_ref,
              sh_ref, pool_ref, y_ref, st_ref):
    tm = x_ref.shape[0]
    seq_tile = pl.program_id(1)

    @pl.when(seq_tile == 0)
    def _():
        sh_ref[0:SHIFT_PAD, :] = jnp.zeros((SHIFT_PAD, sh_ref.shape[1]), F32)
        pool_ref[0:HALO, :] = jnp.zeros((HALO, POOL_WIDTH), F32)
        st_ref[...] = jnp.zeros(st_ref.shape, F32)

    h = _rms(x_ref[...], vd_ref[0:1, :]).astype(BF16)

    sh_ref[SHIFT_PAD:SHIFT_PAD + tm, :] = jnp.dot(
        h, wcat_ref[:, C_RKV:C_SHIFT_END], preferred_element_type=F32)
    pool_ref[HALO:HALO + tm, :] = jnp.dot(
        h, wcat_ref[:, C_POOL:C_GATE], preferred_element_type=F32)
    a1 = jnp.dot(h, wcat_ref[:, C_A1:C_END], preferred_element_type=F32)

    gates = [None] * (2 * D_MODEL // GROUP)
    mixed = [None] * (tm // SUB_ROWS)
    pool_out = {}

    def gate_piece(i):
        def run():
            c0 = C_GATE + i * GROUP
            logits = jnp.dot(h, wcat_ref[:, c0:c0 + GROUP], preferred_element_type=F32)
            bias = vd_ref[2 + (i * GROUP) // D_MODEL:3 + (i * GROUP) // D_MODEL,
                          (i * GROUP) % D_MODEL:(i * GROUP) % D_MODEL + GROUP]
            gates[i] = jax.nn.sigmoid((logits + bias).astype(BF16))
        return run

    def band_piece(j):
        def run():
            r0 = j * SUB_ROWS
            pcur = pool_ref[HALO + r0:HALO + r0 + SUB_ROWS, :]
            phist = pool_ref[r0:r0 + HALO, :]
            t_abs = seq_tile * tm + r0 + lax.broadcasted_iota(jnp.int32, (SUB_ROWS, 1), 0)
            parts = []
            for gi, win in enumerate(POOL_WINDOWS):
                gs = slice(gi * POOL_GROUP, (gi + 1) * POOL_GROUP)
                wsum = jnp.dot(band_ref[gi], pcur[:, gs].astype(BF16), preferred_element_type=F32)
                whist = jnp.dot(bandh_ref[gi], phist[:, gs].astype(BF16), preferred_element_type=F32)
                wsum = jnp.concatenate([wsum[0:HIST_ROWS] + whist, wsum[HIST_ROWS:]], axis=0)
                inv_count = 1.0 / jnp.minimum(t_abs + 1, win).astype(F32)
                parts.append(wsum * inv_count - pcur[:, gs])
            mixed[j] = jnp.concatenate(parts, axis=1)
        return run

    def pool_proj():
        pool_ref[0:HALO, :] = pool_ref[tm:tm + HALO, :]
        m = jnp.concatenate(mixed, axis=0)
        y = jnp.concatenate([_dot(m[:, t * GROUP:(t + 1) * GROUP], wpool_ref[t])
                             for t in range(POOL_WIDTH // GROUP)], axis=1)
        pool_out["y"] = (y * v5_ref[10:11, :]).astype(BF16)

    def pool_branch_piece(i):
        def run():
            half = D_MODEL // 2
            pool_out[i] = jnp.dot(pool_out["y"], wbp_ref[:, i * half:(i + 1) * half],
                                  preferred_element_type=F32).astype(BF16)
        return run

    fillers = ([band_piece(j) for j in range(tm // SUB_ROWS)] + [pool_proj]
               + [pool_branch_piece(0), pool_branch_piece(1)]
               + [gate_piece(i) for i in range(len(gates))])

    def fill(n=1):
        for _ in range(n):
            if fillers:
                fillers.pop(0)()

    rw = RWKV_WIDTH
    mu_r, mu_k, mu_v = v5_ref[0:1, :], v5_ref[1:2, :], v5_ref[2:3, :]
    w0, a0 = v5_ref[3:4, :], v5_ref[4:5, :]
    k_k, k_a, r_k = v5_ref[5:6, :], v5_ref[6:7, :], v5_ref[7:8, :]
    ln_w, ln_b = v5_ref[8:9, :], v5_ref[9:10, :]
    lane = lax.broadcasted_iota(jnp.int32, (1, LORA_WIDTH), 1)

    n_waves = tm // SUB_ROWS
    prep = [None] * n_waves

    def prep_wave(wave):
        r0 = wave * SUB_ROWS
        cur = sh_ref[SHIFT_PAD + r0:SHIFT_PAD + r0 + SUB_ROWS, :]
        prv = sh_ref[SHIFT_PAD - 1 + r0:SHIFT_PAD - 1 + r0 + SUB_ROWS, :]

        def lerp(c0, mu):
            c = cur[:, c0:c0 + rw]
            return c + (prv[:, c0:c0 + rw] - c) * mu

        r = lerp(0, mu_r)
        k = lerp(rw, mu_k)
        v = lerp(2 * rw, mu_v)
        yield

        lpre = a1[r0:r0 + SUB_ROWS] + prv[:, C_A2:C_SHIFT_END]
        lact = jnp.where(lane < DECAY_LORA, jnp.tanh(lpre),
                         jnp.where(lane < DECAY_LORA + AAA_LORA, lpre, jax.nn.sigmoid(lpre)))
        lora = _dot(lact, wb_ref[...])
        ld = -math.exp(-0.5) * jax.nn.sigmoid(w0 + lora[:, 0:rw])
        a = jax.nn.sigmoid(a0 + lora[:, rw:2 * rw])
        g = lora[:, 2 * rw:3 * rw]
        yield

        kk = k * k_k
        kk = kk * lax.rsqrt(jnp.maximum(_head_sum(kk * kk, ones_ref), 1e-24))
        k2 = k * (1.0 + (a - 1.0) * k_a)
        bvec = kk * a
        yield

        ld_hi = ld.astype(BF16)
        ld_lo = (ld - ld_hi.astype(F32)).astype(BF16)
        lc = (jnp.dot(tri_ref[...], ld_hi, preferred_element_type=F32)
              + jnp.dot(tri_ref[...], ld_lo, preferred_element_type=F32))
        e_w = jnp.exp(lc)
        e_wi = jnp.exp(-lc)
        yield
        prep[wave] = dict(at=-kk * jnp.exp(lc - ld), rt=r * e_w, bt=bvec * e_wi, kt=k2 * e_wi,
                          v=v, e_w=e_w, g=g, bonus_in=r * k2 * r_k)

    for _ in prep_wave(0):
        fill()
    fill()

    masks = _wkv_masks()
    n_groups = rw // GROUP
    groups = [slice(q * GROUP, (q + 1) * GROUP) for q in range(n_groups)]
    state = [st_ref[:, gs] for gs in groups]

    def chunk_items(p):
        return [dict(at=p["at"][c0:c0 + CHUNK, gs], rt=p["rt"][c0:c0 + CHUNK, gs],
                     bt=p["bt"][c0:c0 + CHUNK, gs], kt=p["kt"][c0:c0 + CHUNK, gs],
                     v=p["v"][c0:c0 + CHUNK, gs], wc=p["e_w"][c0 + CHUNK - 1:c0 + CHUNK, gs])
                for c0 in range(0, SUB_ROWS, CHUNK) for gs in groups]

    def state_steps(wave, par):
        def step(ci):
            def run():
                for q, gs in enumerate(groups):
                    y, state[q] = _wkv_state_step(par[ci * n_groups + q], state[q], masks)
                    r0 = wave * SUB_ROWS + ci * CHUNK
                    y_ref[r0:r0 + CHUNK, gs] = y
            return run
        return [step(ci) for ci in range(SUB_ROWS // CHUNK)]

    n_gate = D_MODEL // GROUP

    def output_rows(wave):
        rows = slice(wave * SUB_ROWS, (wave + 1) * SUB_ROWS)
        p = prep[wave]
        fill(len(fillers))
        gate0 = jnp.concatenate(gates[0:n_gate], axis=1)
        gate1 = jnp.concatenate(gates[n_gate:2 * n_gate], axis=1)
        y_pool = jnp.concatenate([pool_out[0], pool_out[1]], axis=1)
        y = y_ref[rows, :]
        inv_n = 1.0 / HEAD_SIZE
        mean = _head_sum(y, ones_ref) * inv_n
        yield
        yc = y - mean
        var = _head_sum(yc * yc, ones_ref) * inv_n
        yield
        yn = yc * lax.rsqrt(var + GN_EPS) * ln_w + ln_b
        bonus = _head_sum(p["bonus_in"], ones_ref) * p["v"]
        y_rwkv = (yn + bonus) * p["g"]
        yield
        merged = gate0[rows] * _dot(y_rwkv, wbr_ref[...]).astype(BF16) + gate1[rows] * y_pool[rows]
        yield
        o_ref[rows, :] = x_ref[rows, :] + _rms(_dot(merged, wout_ref[...]), vd_ref[1:2, :])

    pending = []

    def fill_pending():
        if pending:
            pending.pop(0)()

    for wave in range(n_waves):
        next_prep = prep_wave(wave + 1) if wave + 1 < n_waves else iter(())

        def between_stages():
            fill_pending()
            next(next_prep, None)
            fill()

        par = _wkv_parallel(chunk_items(prep[wave]), masks, between_stages)
        for _ in next_prep:
            fill()
        while pending:
            fill_pending()
        pending = state_steps(wave, par)
        if wave > 0:
            for _ in output_rows(wave - 1):
                fill_pending()
    while pending:
        fill_pending()
    for q, gs in enumerate(groups):
        st_ref[:, gs] = state[q]
    sh_ref[SHIFT_PAD - 1:SHIFT_PAD, :] = sh_ref[SHIFT_PAD - 1 + tm:SHIFT_PAD + tm, :]
    for _ in output_rows(n_waves - 1):
        pass


def _mix(x2d, bsz, seq, vd, v5, wcat, wb, wpool, wbr, wbp, wout, tri, band, bandh, ones):
    tm = MIX_ROWS
    d = x2d.shape[1]
    tiles = seq // tm
    consts = (vd, v5, wcat, wb, wpool, wbr, wbp, wout, tri, band, bandh, ones)
    return pl.pallas_call(
        _mix_body,
        grid=(bsz, tiles),
        in_specs=[pl.BlockSpec((tm, d), lambda b, s: (b * tiles + s, 0))]
                 + [_const_spec(c.shape) for c in consts],
        out_specs=pl.BlockSpec((tm, d), lambda b, s: (b * tiles + s, 0)),
        out_shape=jax.ShapeDtypeStruct(x2d.shape, F32),
        scratch_shapes=[
            pltpu.VMEM((SHIFT_PAD + tm, C_SHIFT_END), F32),
            pltpu.VMEM((HALO + tm, POOL_WIDTH), F32),
            pltpu.VMEM((tm, RWKV_WIDTH), F32),
            pltpu.VMEM((HEAD_SIZE, RWKV_WIDTH), F32),
        ],
        compiler_params=pltpu.CompilerParams(
            dimension_semantics=("arbitrary", "arbitrary"), vmem_limit_bytes=VMEM_LIMIT),
        name="token_mixing",
    )(x2d, *consts)


def _mix_constants(tm):
    idx = np.arange(tm)
    tri = ((idx[:, None] // CHUNK == idx[None, :] // CHUNK) & (idx[None, :] <= idx[:, None]))
    band = np.stack([(idx[None, :] <= idx[:, None]) & (idx[None, :] > idx[:, None] - w)
                     for w in POOL_WINDOWS])
    hist = np.arange(HALO) - HALO
    first = np.arange(HIST_ROWS)
    bandh = np.stack([hist[None, :] > first[:, None] - w for w in POOL_WINDOWS])
    hid = np.arange(4 * HEAD_SIZE) // HEAD_SIZE
    ones = hid[:, None] == hid[None, :]
    as_bf16 = lambda m: jnp.asarray(m.astype(np.float32), dtype=BF16)
    return as_bf16(tri), as_bf16(band), as_bf16(bandh), as_bf16(ones)


def _block_diag(blocks):
    n = len(blocks)
    rows = []
    for i, blk in enumerate(blocks):
        rows.append(jnp.concatenate(
            [blk if j == i else jnp.zeros((blk.shape[0], blocks[j].shape[1]), blk.dtype)
             for j in range(n)], axis=1))
    return jnp.concatenate(rows, axis=0)


def kernel(x, norm_gains, ffn1_gate, ffn1_up, ffn1_down, w_in, gate_bias, mu_rkv, mu_wag, w0,
           decay_a, decay_b, a0, aaa_a, aaa_b, gate_a, gate_b, k_k, k_a, r_k, ln_x_w, ln_x_b,
           pool_w, pool_scale, w_branch_rwkv, w_branch_pool, w_out, ffn2_gate, ffn2_up, ffn2_down):
    bsz, seq, d = x.shape
    depth = norm_gains.shape[0]
    tri, band, bandh, ones = _mix_constants(SUB_ROWS)
    x2d = x.reshape(bsz * seq, d)
    for l in range(depth):
        g = norm_gains[l]
        x2d, wg2, wu2, wd2 = _ffn(x2d, g[0:2], ffn1_gate[l].astype(BF16), ffn1_up[l].astype(BF16),
                                  ffn1_down[l].astype(BF16),
                                  cast=(ffn2_gate[l], ffn2_up[l], ffn2_down[l]))

        lora_a = jnp.concatenate([decay_a[l], aaa_a[l], gate_a[l]], axis=1)
        mu = jnp.concatenate([jnp.broadcast_to(mu_wag[l, 0][:, None], decay_a[l].shape),
                              jnp.broadcast_to(mu_wag[l, 1][:, None], aaa_a[l].shape),
                              jnp.broadcast_to(mu_wag[l, 2][:, None], gate_a[l].shape)], axis=1)
        w_split = w_in[l]
        wcat = jnp.concatenate(
            [w_split[:, 0:3 * RWKV_WIDTH], mu * lora_a, w_split[:, 3 * RWKV_WIDTH:],
             (1.0 - mu) * lora_a], axis=1).astype(BF16)
        wb = _block_diag([decay_b[l], aaa_b[l], gate_b[l]]).astype(BF16)
        per_tile = GROUP // POOL_GROUP
        wpool = jnp.stack([_block_diag([pool_w[l, t * per_tile + i] for i in range(per_tile)])
                           for t in range(pool_w.shape[1] // per_tile)]).astype(BF16)
        vd = jnp.stack([g[2], g[3], gate_bias[l, 0], gate_bias[l, 1]])
        v5 = jnp.stack([mu_rkv[l, 0], mu_rkv[l, 1], mu_rkv[l, 2], w0[l], a0[l], k_k[l], k_a[l],
                        r_k[l].reshape(-1), ln_x_w[l], ln_x_b[l], pool_scale[l]])
        x2d = _mix(x2d, bsz, seq, vd, v5, wcat, wb, wpool,
                   w_branch_rwkv[l].astype(BF16), w_branch_pool[l].astype(BF16),
                   w_out[l].astype(BF16), tri, band, bandh, ones)

        x2d, = _ffn(x2d, g[4:6], wg2, wu2, wd2)
    return x2d.reshape(bsz, seq, d)
```

```python
import functools
import math

import numpy as np
import jax
import jax.numpy as jnp
from jax import lax
from jax.experimental import pallas as pl
from jax.experimental.pallas import tpu as pltpu

F32 = jnp.float32
BF16 = jnp.bfloat16

D_MODEL = 1024
RWKV_WIDTH = 512
HEAD_SIZE = 64
POOL_WIDTH = 512
POOL_WINDOWS = (2, 4, 8, 16)
POOL_GROUP = 128
DECAY_LORA = 64
AAA_LORA = 64
GATE_LORA = 128
LORA_WIDTH = DECAY_LORA + AAA_LORA + GATE_LORA
GN_EPS = HEAD_SIZE * 1e-5
RMS_EPS = 1e-6

CHUNK = 64
PAIR = 2 * HEAD_SIZE
GROUP = 4 * HEAD_SIZE
HALO = 128
HIST_ROWS = 16
SHIFT_PAD = 8
BF16_SUBLANES = 16

FFN_ROWS = 1024
FFN_SUB = 256
FFN_COLS = 256
MIX_ROWS = 512
SUB_ROWS = 256

C_RKV = 0
C_A2 = 3 * RWKV_WIDTH
C_SHIFT_END = C_A2 + LORA_WIDTH
C_POOL = C_SHIFT_END
C_GATE = C_POOL + POOL_WIDTH
C_A1 = C_GATE + 2 * D_MODEL
C_END = C_A1 + LORA_WIDTH

VMEM_LIMIT = 56 * 1024 * 1024


def _dot(a, b):
    return jnp.dot(a.astype(BF16), b.astype(BF16), preferred_element_type=F32)


def _rms(x, g):
    ms = jnp.mean(x * x, axis=-1, keepdims=True)
    return x * lax.rsqrt(ms + RMS_EPS) * g


def _ffn_body(*refs, n_cast, emit_next):
    x_ref, gains_ref, wg_ref, wu_ref, wd_ref = refs[:5]
    cast_in = refs[5:5 + n_cast]
    o_ref = refs[5 + n_cast]
    n_main = 2 if emit_next else 1
    next_ref = refs[6 + n_cast] if emit_next else None
    cast_out = refs[5 + n_cast + n_main:]
    for src, dst in zip(cast_in, cast_out):
        dst[...] = src[...].astype(BF16)

    n_sub = x_ref.shape[0] // FFN_SUB
    d_ff = wg_ref.shape[1]

    def prologue(i):
        return _rms(x_ref[i * FFN_SUB:(i + 1) * FFN_SUB, :], gains_ref[0:1, :]).astype(BF16)

    def epilogue(i, f):
        rows = slice(i * FFN_SUB, (i + 1) * FFN_SUB)
        y = x_ref[rows, :] + 0.5 * _rms(f, gains_ref[1:2, :])
        o_ref[rows, :] = y
        if emit_next:
            next_ref[rows, :] = _rms(y, gains_ref[2:3, :]).astype(BF16)

    h = prologue(0)
    f_prev = None
    for i in range(n_sub):
        acts = []
        h_next = None
        for c0 in range(0, d_ff, FFN_COLS):
            gate = jnp.dot(h, wg_ref[:, c0:c0 + FFN_COLS], preferred_element_type=F32)
            up = jnp.dot(h, wu_ref[:, c0:c0 + FFN_COLS], preferred_element_type=F32)
            acts.append((gate * jax.nn.sigmoid(gate) * up).astype(BF16))
            if c0 == 0 and i + 1 < n_sub:
                h_next = prologue(i + 1)
            if c0 == 2 * FFN_COLS and f_prev is not None:
                epilogue(i - 1, f_prev)
        f_prev = jnp.dot(jnp.concatenate(acts, axis=1), wd_ref[...], preferred_element_type=F32)
        h = h_next
    epilogue(n_sub - 1, f_prev)


def _const_spec(shape):
    return pl.BlockSpec(shape, lambda *_: (0,) * len(shape), pipeline_mode=pl.Buffered(1))


def _cast_rows(rows, steps):
    for blk in range(-(-rows // steps), rows + 1):
        if blk % BF16_SUBLANES == 0 and rows % blk == 0:
            return blk
    raise ValueError(f"no bf16-aligned row block covers {rows} rows in {steps} steps")


def _ffn(x2d, gains, wg, wu, wd, cast=()):
    t, d = x2d.shape
    emit_next = gains.shape[0] == 3
    row_spec = pl.BlockSpec((FFN_ROWS, d), lambda i: (i, 0))
    steps = t // FFN_ROWS
    cast_specs = []
    for w in cast:
        blk = _cast_rows(w.shape[0], steps)
        last = w.shape[0] // blk - 1
        cast_specs.append(pl.BlockSpec(
            (blk, w.shape[1]), functools.partial(lambda i, last: (jnp.minimum(i, last), 0), last=last)))
    out = pl.pallas_call(
        functools.partial(_ffn_body, n_cast=len(cast), emit_next=emit_next),
        grid=(steps,),
        in_specs=[row_spec] + [_const_spec(c.shape) for c in (gains, wg, wu, wd)] + cast_specs,
        out_specs=[row_spec] + ([row_spec] if emit_next else []) + cast_specs,
        out_shape=[jax.ShapeDtypeStruct((t, d), F32)]
                  + ([jax.ShapeDtypeStruct((t, d), BF16)] if emit_next else [])
                  + [jax.ShapeDtypeStruct(w.shape, BF16) for w in cast],
        compiler_params=pltpu.CompilerParams(
            dimension_semantics=("arbitrary",), vmem_limit_bytes=VMEM_LIMIT),
        name="macaron_ffn",
    )(x2d, gains, wg, wu, wd, *cast)
    return out


def _head_sum(x, ones_ref):
    half = ones_ref.shape[0]
    parts = [_dot(x[:, c:c + half], ones_ref[...]) for c in range(0, x.shape[1], half)]
    return jnp.concatenate(parts, axis=1)


def _wkv_masks():
    row = lax.broadcasted_iota(jnp.int32, (2 * CHUNK, 2 * GROUP), 0)
    col = lax.broadcasted_iota(jnp.int32, (2 * CHUNK, 2 * GROUP), 1)
    lane = lax.broadcasted_iota(jnp.int32, (1, PAIR), 1)
    m0 = (lane < HEAD_SIZE).astype(F32)
    return dict(
        lo=m0.astype(BF16), hi=(1.0 - m0).astype(BF16),
        tri=(col % CHUNK) < (row % CHUNK) + jnp.where(row < CHUNK, 0, 1),
        eye=(lax.broadcasted_iota(jnp.int32, (CHUNK, GROUP), 0)
             == lax.broadcasted_iota(jnp.int32, (CHUNK, GROUP), 1) % CHUNK).astype(F32),
        m0=jnp.concatenate([m0, m0], axis=1),
        m1=jnp.concatenate([1.0 - m0, 1.0 - m0], axis=1))


def _bd(x, masks):
    xb = x.astype(BF16)
    zero = jnp.zeros((x.shape[0], PAIR), BF16)
    blocks = []
    for p in range(GROUP // PAIR):
        tile = xb[:, p * PAIR:(p + 1) * PAIR]
        for m in (masks["lo"], masks["hi"]):
            row = [zero] * (GROUP // PAIR)
            row[p] = tile * m
            blocks.append(jnp.concatenate(row, axis=1))
    return jnp.concatenate(blocks, axis=0)


def _wkv_parallel(chunks, masks, fill):
    items = range(len(chunks))
    bd = lambda x: _bd(x, masks)
    eye = masks["eye"]
    a_n = [c["at"] for c in chunks]
    r_n = [c["rt"] for c in chunks]
    b_n = [c["bt"] for c in chunks]
    k_n = [c["kt"] for c in chunks]
    v_n = [c["v"] for c in chunks]
    wc = [c["wc"] for c in chunks]

    o = [jnp.where(masks["tri"],
                   _dot(jnp.concatenate([a_n[i], r_n[i]], axis=0),
                        jnp.concatenate([bd(b_n[i]).T, bd(k_n[i]).T], axis=1)), 0.0)
         for i in items]
    fill()
    av = [_dot(o[i][:, GROUP:2 * GROUP], bd(v_n[i])) for i in items]
    fill()

    lk = [o[i][0:CHUNK, 0:GROUP] for i in items]
    arb = [o[i][CHUNK:2 * CHUNK, 0:GROUP] for i in items]
    t_inv = [eye + lk[i] for i in items]
    prod = [_dot(jnp.concatenate([lk[i], arb[i]], axis=0), bd(lk[i])) for i in items]
    lk = [prod[i][0:CHUNK] for i in items]
    arb_t = [arb[i] + prod[i][CHUNK:2 * CHUNK] for i in items]
    fill()
    n_sq = CHUNK.bit_length() - 2
    for _ in range(n_sq):
        prod = [_dot(jnp.concatenate([lk[i], t_inv[i], arb_t[i]], axis=0), bd(lk[i])) for i in items]
        lk = [prod[i][0:CHUNK] for i in items]
        t_inv = [t_inv[i] + prod[i][CHUNK:2 * CHUNK] for i in items]
        arb_t = [arb_t[i] + prod[i][2 * CHUNK:3 * CHUNK] for i in items]
        fill()
    prod = [_dot(jnp.concatenate([t_inv[i], arb_t[i]], axis=0), bd(lk[i])) for i in items]
    t_inv = [t_inv[i] + prod[i][0:CHUNK] for i in items]
    arb_t = [arb_t[i] + prod[i][CHUNK:2 * CHUNK] for i in items]
    fill()

    pq = [_dot(jnp.concatenate([t_inv[i], arb_t[i]], axis=0),
               jnp.concatenate([bd(a_n[i]), bd(av[i][0:CHUNK])], axis=1)) for i in items]
    fill()

    out = []
    for i in items:
        bkw_t = jnp.concatenate([b_n[i] * wc[i], k_n[i] * wc[i]], axis=0).astype(BF16).T
        g_parts, h_parts = [], []
        for pp in range(GROUP // PAIR):
            ls = slice(pp * PAIR, (pp + 1) * PAIR)
            p_p = pq[i][0:CHUNK, ls]
            q_p = pq[i][0:CHUNK, GROUP + pp * PAIR:GROUP + (pp + 1) * PAIR]
            w_p = jnp.concatenate(
                [jnp.concatenate([p_p, q_p], axis=1),
                 jnp.concatenate([jnp.zeros_like(p_p), v_n[i][:, ls]], axis=1)], axis=0)
            gh = _dot(bkw_t[ls, :], w_p)
            sel = gh[0:HEAD_SIZE] * masks["m0"] + gh[HEAD_SIZE:PAIR] * masks["m1"]
            g_parts.append(sel[:, 0:PAIR])
            h_parts.append(sel[:, PAIR:2 * PAIR])
        out.append(dict(
            g=jnp.concatenate(g_parts, axis=1) + eye * wc[i],
            h=jnp.concatenate(h_parts, axis=1),
            ry=r_n[i] + pq[i][CHUNK:2 * CHUNK, 0:GROUP],
            yc=pq[i][CHUNK:2 * CHUNK, GROUP:2 * GROUP] + av[i][CHUNK:2 * CHUNK]))
    fill()
    return out


def _wkv_state_step(par, s_nat, masks):
    out = _dot(jnp.concatenate([par["g"], par["ry"]], axis=0), _bd(s_nat, masks))
    return out[CHUNK:2 * CHUNK] + par["yc"], out[0:CHUNK] + par["h"]


def _mix_body(x_ref, h_ref, vd_ref, v5_ref, wcat_ref, wb_ref, wpool_ref, wbr_ref, wbp_ref, wout_ref,
              tri_ref, band_ref, bandh_ref, ones_ref, o_ref,
              sh_ref, pool_ref, y_ref, st_ref):
    tm = x_ref.shape[0]
    seq_tile = pl.program_id(1)

    @pl.when(seq_tile == 0)
    def _():
        sh_ref[0:SHIFT_PAD, :] = jnp.zeros((SHIFT_PAD, sh_ref.shape[1]), F32)
        pool_ref[0:HALO, :] = jnp.zeros((HALO, POOL_WIDTH), F32)
        st_ref[...] = jnp.zeros(st_ref.shape, F32)

    h = h_ref[...]

    sh_ref[SHIFT_PAD:SHIFT_PAD + tm, :] = jnp.dot(
        h, wcat_ref[:, C_RKV:C_SHIFT_END], preferred_element_type=F32)
    pool_ref[HALO:HALO + tm, :] = jnp.dot(
        h, wcat_ref[:, C_POOL:C_GATE], preferred_element_type=F32)
    a1 = jnp.dot(h, wcat_ref[:, C_A1:C_END], preferred_element_type=F32)

    gates = [None] * (2 * D_MODEL // GROUP)
    mixed = [None] * (tm // SUB_ROWS)
    pool_out = {}

    def gate_piece(i):
        def run():
            c0 = C_GATE + i * GROUP
            logits = jnp.dot(h, wcat_ref[:, c0:c0 + GROUP], preferred_element_type=F32)
            bias = vd_ref[1 + (i * GROUP) // D_MODEL:2 + (i * GROUP) // D_MODEL,
                          (i * GROUP) % D_MODEL:(i * GROUP) % D_MODEL + GROUP]
            gates[i] = jax.nn.sigmoid((logits + bias).astype(BF16))
        return run

    def band_piece(j):
        def run():
            r0 = j * SUB_ROWS
            pcur = pool_ref[HALO + r0:HALO + r0 + SUB_ROWS, :]
            phist = pool_ref[r0:r0 + HALO, :]
            t_abs = seq_tile * tm + r0 + lax.broadcasted_iota(jnp.int32, (SUB_ROWS, 1), 0)
            parts = []
            for gi, win in enumerate(POOL_WINDOWS):
                gs = slice(gi * POOL_GROUP, (gi + 1) * POOL_GROUP)
                wsum = jnp.dot(band_ref[gi], pcur[:, gs].astype(BF16), preferred_element_type=F32)
                whist = jnp.dot(bandh_ref[gi], phist[:, gs].astype(BF16), preferred_element_type=F32)
                wsum = jnp.concatenate([wsum[0:HIST_ROWS] + whist, wsum[HIST_ROWS:]], axis=0)
                inv_count = 1.0 / jnp.minimum(t_abs + 1, win).astype(F32)
                parts.append(wsum * inv_count - pcur[:, gs])
            mixed[j] = jnp.concatenate(parts, axis=1)
        return run

    def pool_proj():
        pool_ref[0:HALO, :] = pool_ref[tm:tm + HALO, :]
        m = jnp.concatenate(mixed, axis=0)
        y = jnp.concatenate([_dot(m[:, t * GROUP:(t + 1) * GROUP], wpool_ref[t])
                             for t in range(POOL_WIDTH // GROUP)], axis=1)
        pool_out["y"] = (y * v5_ref[10:11, :]).astype(BF16)

    def pool_branch_piece(i):
        def run():
            half = D_MODEL // 2
            pool_out[i] = jnp.dot(pool_out["y"], wbp_ref[:, i * half:(i + 1) * half],
                                  preferred_element_type=F32).astype(BF16)
        return run

    fillers = ([band_piece(j) for j in range(tm // SUB_ROWS)] + [pool_proj]
               + [pool_branch_piece(0), pool_branch_piece(1)]
               + [gate_piece(i) for i in range(len(gates))])

    def fill(n=1):
        for _ in range(n):
            if fillers:
                fillers.pop(0)()

    rw = RWKV_WIDTH
    mu_r, mu_k, mu_v = v5_ref[0:1, :], v5_ref[1:2, :], v5_ref[2:3, :]
    w0, a0 = v5_ref[3:4, :], v5_ref[4:5, :]
    k_k, k_a, r_k = v5_ref[5:6, :], v5_ref[6:7, :], v5_ref[7:8, :]
    ln_w, ln_b = v5_ref[8:9, :], v5_ref[9:10, :]
    lane = lax.broadcasted_iota(jnp.int32, (1, LORA_WIDTH), 1)

    n_waves = tm // SUB_ROWS
    prep = [None] * n_waves

    def prep_wave(wave):
        r0 = wave * SUB_ROWS
        cur = sh_ref[SHIFT_PAD + r0:SHIFT_PAD + r0 + SUB_ROWS, :]
        prv = sh_ref[SHIFT_PAD - 1 + r0:SHIFT_PAD - 1 + r0 + SUB_ROWS, :]

        def lerp(c0, mu):
            c = cur[:, c0:c0 + rw]
            return c + (prv[:, c0:c0 + rw] - c) * mu

        r = lerp(0, mu_r)
        k = lerp(rw, mu_k)
        v = lerp(2 * rw, mu_v)
        yield

        lpre = a1[r0:r0 + SUB_ROWS] + prv[:, C_A2:C_SHIFT_END]
        lact = jnp.where(lane < DECAY_LORA, jnp.tanh(lpre),
                         jnp.where(lane < DECAY_LORA + AAA_LORA, lpre, jax.nn.sigmoid(lpre)))
        lora = _dot(lact, wb_ref[...])
        ld = -math.exp(-0.5) * jax.nn.sigmoid(w0 + lora[:, 0:rw])
        a = jax.nn.sigmoid(a0 + lora[:, rw:2 * rw])
        g = lora[:, 2 * rw:3 * rw]
        yield

        kk = k * k_k
        kk = kk * lax.rsqrt(jnp.maximum(_head_sum(kk * kk, ones_ref), 1e-24))
        k2 = k * (1.0 + (a - 1.0) * k_a)
        bvec = kk * a
        yield

        ld_hi = ld.astype(BF16)
        ld_lo = (ld - ld_hi.astype(F32)).astype(BF16)
        lc = (jnp.dot(tri_ref[...], ld_hi, preferred_element_type=F32)
              + jnp.dot(tri_ref[...], ld_lo, preferred_element_type=F32))
        e_w = jnp.exp(lc)
        e_wi = jnp.exp(-lc)
        yield
        prep[wave] = dict(at=-kk * jnp.exp(lc - ld), rt=r * e_w, bt=bvec * e_wi, kt=k2 * e_wi,
                          v=v, e_w=e_w, g=g, bonus_in=r * k2 * r_k)

    for _ in prep_wave(0):
        fill()
    fill()

    masks = _wkv_masks()
    n_groups = rw // GROUP
    groups = [slice(q * GROUP, (q + 1) * GROUP) for q in range(n_groups)]
    state = [st_ref[:, gs] for gs in groups]

    def chunk_items(p):
        return [dict(at=p["at"][c0:c0 + CHUNK, gs], rt=p["rt"][c0:c0 + CHUNK, gs],
                     bt=p["bt"][c0:c0 + CHUNK, gs], kt=p["kt"][c0:c0 + CHUNK, gs],
                     v=p["v"][c0:c0 + CHUNK, gs], wc=p["e_w"][c0 + CHUNK - 1:c0 + CHUNK, gs])
                for c0 in range(0, SUB_ROWS, CHUNK) for gs in groups]

    def state_steps(wave, par):
        def step(ci):
            def run():
                for q, gs in enumerate(groups):
                    y, state[q] = _wkv_state_step(par[ci * n_groups + q], state[q], masks)
                    r0 = wave * SUB_ROWS + ci * CHUNK
                    y_ref[r0:r0 + CHUNK, gs] = y
            return run
        return [step(ci) for ci in range(SUB_ROWS // CHUNK)]

    n_gate = D_MODEL // GROUP

    def output_rows(wave):
        rows = slice(wave * SUB_ROWS, (wave + 1) * SUB_ROWS)
        p = prep[wave]
        fill(len(fillers))
        gate0 = jnp.concatenate(gates[0:n_gate], axis=1)
        gate1 = jnp.concatenate(gates[n_gate:2 * n_gate], axis=1)
        y_pool = jnp.concatenate([pool_out[0], pool_out[1]], axis=1)
        y = y_ref[rows, :]
        inv_n = 1.0 / HEAD_SIZE
        mean = _head_sum(y, ones_ref) * inv_n
        yield
        yc = y - mean
        var = _head_sum(yc * yc, ones_ref) * inv_n
        yield
        yn = yc * lax.rsqrt(var + GN_EPS) * ln_w + ln_b
        bonus = _head_sum(p["bonus_in"], ones_ref) * p["v"]
        y_rwkv = (yn + bonus) * p["g"]
        yield
        merged = gate0[rows] * _dot(y_rwkv, wbr_ref[...]).astype(BF16) + gate1[rows] * y_pool[rows]
        yield
        o_ref[rows, :] = x_ref[rows, :] + _rms(_dot(merged, wout_ref[...]), vd_ref[0:1, :])

    pending = []

    def fill_pending():
        if pending:
            pending.pop(0)()

    for wave in range(n_waves):
        next_prep = prep_wave(wave + 1) if wave + 1 < n_waves else iter(())

        def between_stages():
            fill_pending()
            next(next_prep, None)
            fill()

        par = _wkv_parallel(chunk_items(prep[wave]), masks, between_stages)
        for _ in next_prep:
            fill()
        while pending:
            fill_pending()
        pending = state_steps(wave, par)
        if wave > 0:
            for _ in output_rows(wave - 1):
                fill_pending()
    while pending:
        fill_pending()
    for q, gs in enumerate(groups):
        st_ref[:, gs] = state[q]
    sh_ref[SHIFT_PAD - 1:SHIFT_PAD, :] = sh_ref[SHIFT_PAD - 1 + tm:SHIFT_PAD + tm, :]
    for _ in output_rows(n_waves - 1):
        pass


def _mix(x2d, h2d, bsz, seq, vd, v5, wcat, wb, wpool, wbr, wbp, wout, tri, band, bandh, ones):
    tm = MIX_ROWS
    d = x2d.shape[1]
    tiles = seq // tm
    consts = (vd, v5, wcat, wb, wpool, wbr, wbp, wout, tri, band, bandh, ones)
    return pl.pallas_call(
        _mix_body,
        grid=(bsz, tiles),
        in_specs=[pl.BlockSpec((tm, d), lambda b, s: (b * tiles + s, 0))] * 2
                 + [_const_spec(c.shape) for c in consts],
        out_specs=pl.BlockSpec((tm, d), lambda b, s: (b * tiles + s, 0)),
        out_shape=jax.ShapeDtypeStruct(x2d.shape, F32),
        scratch_shapes=[
            pltpu.VMEM((SHIFT_PAD + tm, C_SHIFT_END), F32),
            pltpu.VMEM((HALO + tm, POOL_WIDTH), F32),
            pltpu.VMEM((tm, RWKV_WIDTH), F32),
            pltpu.VMEM((HEAD_SIZE, RWKV_WIDTH), F32),
        ],
        compiler_params=pltpu.CompilerParams(
            dimension_semantics=("arbitrary", "arbitrary"), vmem_limit_bytes=VMEM_LIMIT),
        name="token_mixing",
    )(x2d, h2d, *consts)


def _mix_constants(tm):
    idx = np.arange(tm)
    tri = ((idx[:, None] // CHUNK == idx[None, :] // CHUNK) & (idx[None, :] <= idx[:, None]))
    band = np.stack([(idx[None, :] <= idx[:, None]) & (idx[None, :] > idx[:, None] - w)
                     for w in POOL_WINDOWS])
    hist = np.arange(HALO) - HALO
    first = np.arange(HIST_ROWS)
    bandh = np.stack([hist[None, :] > first[:, None] - w for w in POOL_WINDOWS])
    hid = np.arange(4 * HEAD_SIZE) // HEAD_SIZE
    ones = hid[:, None] == hid[None, :]
    as_bf16 = lambda m: jnp.asarray(m.astype(np.float32), dtype=BF16)
    return as_bf16(tri), as_bf16(band), as_bf16(bandh), as_bf16(ones)


def _block_diag(blocks):
    n = len(blocks)
    rows = []
    for i, blk in enumerate(blocks):
        rows.append(jnp.concatenate(
            [blk if j == i else jnp.zeros((blk.shape[0], blocks[j].shape[1]), blk.dtype)
             for j in range(n)], axis=1))
    return jnp.concatenate(rows, axis=0)


def kernel(x, norm_gains, ffn1_gate, ffn1_up, ffn1_down, w_in, gate_bias, mu_rkv, mu_wag, w0,
           decay_a, decay_b, a0, aaa_a, aaa_b, gate_a, gate_b, k_k, k_a, r_k, ln_x_w, ln_x_b,
           pool_w, pool_scale, w_branch_rwkv, w_branch_pool, w_out, ffn2_gate, ffn2_up, ffn2_down):
    bsz, seq, d = x.shape
    depth = norm_gains.shape[0]
    tri, band, bandh, ones = _mix_constants(SUB_ROWS)
    x2d = x.reshape(bsz * seq, d)
    for l in range(depth):
        g = norm_gains[l]
        x2d, h2d, wg2, wu2, wd2 = _ffn(x2d, g[0:3], ffn1_gate[l].astype(BF16), ffn1_up[l].astype(BF16),
                                  ffn1_down[l].astype(BF16),
                                  cast=(ffn2_gate[l], ffn2_up[l], ffn2_down[l]))

        lora_a = jnp.concatenate([decay_a[l], aaa_a[l], gate_a[l]], axis=1)
        mu = jnp.concatenate([jnp.broadcast_to(mu_wag[l, 0][:, None], decay_a[l].shape),
                              jnp.broadcast_to(mu_wag[l, 1][:, None], aaa_a[l].shape),
                              jnp.broadcast_to(mu_wag[l, 2][:, None], gate_a[l].shape)], axis=1)
        w_split = w_in[l]
        wcat = jnp.concatenate(
            [w_split[:, 0:3 * RWKV_WIDTH], mu * lora_a, w_split[:, 3 * RWKV_WIDTH:],
             (1.0 - mu) * lora_a], axis=1).astype(BF16)
        wb = _block_diag([decay_b[l], aaa_b[l], gate_b[l]]).astype(BF16)
        per_tile = GROUP // POOL_GROUP
        wpool = jnp.stack([_block_diag([pool_w[l, t * per_tile + i] for i in range(per_tile)])
                           for t in range(pool_w.shape[1] // per_tile)]).astype(BF16)
        vd = jnp.stack([g[3], gate_bias[l, 0], gate_bias[l, 1]])
        v5 = jnp.stack([mu_rkv[l, 0], mu_rkv[l, 1], mu_rkv[l, 2], w0[l], a0[l], k_k[l], k_a[l],
                        r_k[l].reshape(-1), ln_x_w[l], ln_x_b[l], pool_scale[l]])
        x2d = _mix(x2d, h2d, bsz, seq, vd, v5, wcat, wb, wpool,
                   w_branch_rwkv[l].astype(BF16), w_branch_pool[l].astype(BF16),
                   w_out[l].astype(BF16), tri, band, bandh, ones)

        x2d, = _ffn(x2d, g[4:6], wg2, wu2, wd2)
    return x2d.reshape(bsz, seq, d)
```
